```python
import math
import jax, jax.numpy as jnp
from jax import lax
import numpy as np

D_MODEL = 1024
BATCH = 8
SEQ = 4096
DEPTH = 1
DEC_BATCH = 8
DEC_SEQ = 8192
PAST_LEN = 128

MLA_HEADS = 8
MLA_NOPE = 64
MLA_ROPE = 32
MLA_QK = MLA_NOPE + MLA_ROPE
MLA_V = 64
Q_LORA = 384
KV_LORA = 256
ROPE_THETA = 10000.0
Q_BLOCK = 128
GDN_HEADS = 4
GDN_DK = 128
GDN_DV = 128
CONV_K = 5
CHUNK = 64
D_FF = 2816
EPS = 1e-6

MLA_OUT = MLA_HEADS * MLA_V
GDN_QK = GDN_HEADS * GDN_DK
GDN_OUT = GDN_HEADS * GDN_DV
GDN_CONV_CH = 2 * GDN_QK + GDN_OUT
MIX_WIDTH = MLA_OUT + GDN_OUT
IN_SPLITS = (Q_LORA,
             Q_LORA + KV_LORA,
             Q_LORA + KV_LORA + MLA_ROPE,
             Q_LORA + KV_LORA + MLA_ROPE + GDN_CONV_CH,
             Q_LORA + KV_LORA + MLA_ROPE + GDN_CONV_CH + GDN_OUT,
             Q_LORA + KV_LORA + MLA_ROPE + GDN_CONV_CH + GDN_OUT + 2 * GDN_HEADS)
IN_COLS = Q_LORA + KV_LORA + MLA_ROPE + GDN_CONV_CH + GDN_OUT + 4 * GDN_HEADS

kernel_name = "hybrid_mla_gdn_macaron_encoder"


def rmsnorm(x, g):
    xf = x.astype(jnp.float32)
    y = xf * lax.rsqrt(jnp.mean(xf * xf, axis=-1, keepdims=True) + EPS)
    return (y * g.astype(jnp.float32)).astype(x.dtype)


def l2norm(t):
    return t * lax.rsqrt(jnp.sum(t * t, axis=-1, keepdims=True) + EPS)


def swiglu(x, w_gate, w_up, w_down):
    return (jax.nn.silu(x @ w_gate) * (x @ w_up)) @ w_down


def rope_tables(L):
    inv = ROPE_THETA ** (-jnp.arange(0, MLA_ROPE, 2, dtype=jnp.float32) / MLA_ROPE)
    ang = jnp.arange(L, dtype=jnp.float32)[:, None] * inv[None, :]
    return jnp.cos(ang), jnp.sin(ang)


def apply_rope(x, cos, sin):
    xf = x.astype(jnp.float32)
    x1, x2 = jnp.split(xf, 2, axis=-1)
    c = cos[None, :, None, :]
    s = sin[None, :, None, :]
    return jnp.concatenate([x1 * c - x2 * s, x1 * s + x2 * c], axis=-1).astype(x.dtype)


def mla_mixer(c_q, c_kv, k_rope, q_norm_g, w_uq, kv_norm_g, w_ukv, out_norm_g):
    B, L, _ = c_q.shape
    q = (rmsnorm(c_q, q_norm_g) @ w_uq).reshape(B, L, MLA_HEADS, MLA_QK)
    q_nope, q_pe = jnp.split(q, [MLA_NOPE], axis=-1)
    kv = (rmsnorm(c_kv, kv_norm_g) @ w_ukv).reshape(B, L, MLA_HEADS, MLA_NOPE + MLA_V)
    k_nope, v = jnp.split(kv, [MLA_NOPE], axis=-1)
    cos, sin = rope_tables(L)
    q_pe = apply_rope(q_pe, cos, sin)
    k_pe = apply_rope(k_rope[:, :, None, :], cos, sin)
    q = jnp.concatenate([q_nope, q_pe], axis=-1)
    k = jnp.concatenate([k_nope, jnp.broadcast_to(k_pe, (B, L, MLA_HEADS, MLA_ROPE))], axis=-1)
    scale = MLA_QK ** -0.5
    nb = L // Q_BLOCK
    qb = q.reshape(B, nb, Q_BLOCK, MLA_HEADS, MLA_QK).swapaxes(0, 1)

    def attend(qi):
        s = jnp.einsum('bqhd,bkhd->bhqk', qi, k, preferred_element_type=jnp.float32) * scale
        p = jax.nn.softmax(s, axis=-1).astype(v.dtype)
        return jnp.einsum('bhqk,bkhd->bqhd', p, v)

    o = lax.map(attend, qb)
    o = o.swapaxes(0, 1).reshape(B, L, MLA_OUT)
    return rmsnorm(o, out_norm_g)


def gdn_chunked(q, k, v, g, beta):
    B, L, H, dk = q.shape
    dv = v.shape[-1]
    N = L // CHUNK

    def blocks(t):
        return t.reshape(B, N, CHUNK, H, t.shape[-1]).transpose(1, 0, 3, 2, 4)

    q, k, v = blocks(q), blocks(k), blocks(v)
    g = g.reshape(B, N, CHUNK, H).transpose(1, 0, 3, 2)
    beta = beta.reshape(B, N, CHUNK, H).transpose(1, 0, 3, 2)
    gc = jnp.cumsum(g, axis=-1)
    idx = jnp.arange(CHUNK)
    incl = idx[:, None] >= idx[None, :]
    strict = idx[:, None] > idx[None, :]
    diff = gc[..., :, None] - gc[..., None, :]
    decay = jnp.where(incl, jnp.exp(jnp.where(incl, diff, 0.0)), 0.0)
    kb = k * beta[..., None]
    A = jnp.where(strict, jnp.einsum('nbhid,nbhjd->nbhij', kb, k) * decay, 0.0)
    eye = jnp.eye(CHUNK, dtype=jnp.float32)
    T = lax.linalg.triangular_solve(eye + A, jnp.broadcast_to(eye, A.shape),
                                    left_side=True, lower=True, unit_diagonal=True)
    w = jnp.einsum('nbhij,nbhjd->nbhid', T, kb * jnp.exp(gc)[..., None])
    u = jnp.einsum('nbhij,nbhjd->nbhid', T, v * beta[..., None])
    qk = jnp.where(incl, jnp.einsum('nbhid,nbhjd->nbhij', q, k) * decay, 0.0)
    g_last = gc[..., -1]
    q_g = q * jnp.exp(gc)[..., None]
    k_g = k * jnp.exp(g_last[..., None] - gc)[..., None]
    d_last = jnp.exp(g_last)

    def step(S, xs):
        u_n, w_n, q_n, qk_n, k_n, dl = xs
        v_new = u_n - jnp.einsum('bhcd,bhde->bhce', w_n, S)
        o = jnp.einsum('bhcd,bhde->bhce', q_n, S) + jnp.einsum('bhij,bhje->bhie', qk_n, v_new)
        S = S * dl[..., None, None] + jnp.einsum('bhcd,bhce->bhde', k_n, v_new)
        return S, o

    S0 = jnp.zeros((B, H, dk, dv), jnp.float32)
    _, o = lax.scan(step, S0, (u, w, q_g, qk, k_g, d_last))
    return o.transpose(1, 0, 3, 2, 4).reshape(B, L, H, dv)


def gdn_mixer(qkv, z, a, b, conv_w, a_log, dt_bias, out_norm_g):
    B, L, _ = qkv.shape
    dtype = qkv.dtype
    qkv = lax.conv_general_dilated(qkv, conv_w[:, None, :].astype(dtype), window_strides=(1,),
                                   padding=[(CONV_K // 2, CONV_K // 2)],
                                   dimension_numbers=('NWC', 'WIO', 'NWC'),
                                   feature_group_count=GDN_CONV_CH)
    qkv = jax.nn.silu(qkv)
    q, k, v = jnp.split(qkv, [GDN_QK, 2 * GDN_QK], axis=-1)
    q = l2norm(q.reshape(B, L, GDN_HEADS, GDN_DK).astype(jnp.float32)) * (GDN_DK ** -0.5)
    k = l2norm(k.reshape(B, L, GDN_HEADS, GDN_DK).astype(jnp.float32))
    v = v.reshape(B, L, GDN_HEADS, GDN_DV).astype(jnp.float32)
    a = a.astype(jnp.float32).reshape(B, L, 2, GDN_HEADS)
    b = b.astype(jnp.float32).reshape(B, L, 2, GDN_HEADS)
    g = -jnp.exp(a_log.astype(jnp.float32)) * jax.nn.softplus(a + dt_bias.astype(jnp.float32))
    beta = jax.nn.sigmoid(b)
    o_f = gdn_chunked(q, k, v, g[:, :, 0], beta[:, :, 0])
    flip = lambda t: jnp.flip(t, axis=1)
    o_b = flip(gdn_chunked(flip(q), flip(k), flip(v), flip(g[:, :, 1]), flip(beta[:, :, 1])))
    o = rmsnorm(o_f + o_b, out_norm_g) * jax.nn.silu(z.reshape(B, L, GDN_HEADS, GDN_DV).astype(jnp.float32))
    return o.reshape(B, L, GDN_OUT).astype(dtype)


def encoder_layer(h, ffn1_pre_g, ffn1_w_gate, ffn1_w_up, ffn1_w_down, ffn1_post_g,
                  mix_pre_g, w_in, mla_q_norm_g, mla_w_uq, mla_kv_norm_g, mla_w_ukv, mla_out_norm_g,
                  gdn_conv_w, gdn_a_log, gdn_dt_bias, gdn_out_norm_g, w_out, mix_post_g,
                  ffn2_pre_g, ffn2_w_gate, ffn2_w_up, ffn2_w_down, ffn2_post_g, final_norm_g):
    h = h + 0.5 * rmsnorm(swiglu(rmsnorm(h, ffn1_pre_g), ffn1_w_gate, ffn1_w_up, ffn1_w_down), ffn1_post_g)
    proj = rmsnorm(h, mix_pre_g) @ w_in
    c_q, c_kv, k_rope, qkv, z, a, b = jnp.split(proj, IN_SPLITS, axis=-1)
    y_a = mla_mixer(c_q, c_kv, k_rope, mla_q_norm_g, mla_w_uq, mla_kv_norm_g, mla_w_ukv, mla_out_norm_g)
    y_b = gdn_mixer(qkv, z, a, b, gdn_conv_w, gdn_a_log, gdn_dt_bias, gdn_out_norm_g)
    mix = jnp.concatenate([y_a, y_b], axis=-1) @ w_out
    h = h + rmsnorm(mix, mix_post_g)
    h = h + 0.5 * rmsnorm(swiglu(rmsnorm(h, ffn2_pre_g), ffn2_w_gate, ffn2_w_up, ffn2_w_down), ffn2_post_g)
    return rmsnorm(h, final_norm_g)


def setup_inputs(seed: int = 0) -> dict:
    key = jax.random.key(seed)
    ks = jax.random.split(key, 32)
    f32 = jnp.float32

    def normal(k, shape, scale):
        return jax.random.normal(k, shape, f32) * scale

    def gain(k, n):
        return 1.0 + 0.1 * jax.random.normal(k, (DEPTH, n), f32)

    dt = jnp.exp(jax.random.uniform(ks[20], (DEPTH, 2, GDN_HEADS), f32, math.log(1e-3), math.log(1e-1)))
    return {
        "x_prompt": normal(ks[0], (BATCH, SEQ, D_MODEL), 1.0),
        "x_sample": normal(ks[1], (DEC_BATCH, DEC_SEQ, D_MODEL), 1.0),
        "ffn1_pre_g": gain(ks[2], D_MODEL),
        "ffn1_w_gate": normal(ks[3], (DEPTH, D_MODEL, D_FF), D_MODEL ** -0.5),
        "ffn1_w_up": normal(ks[4], (DEPTH, D_MODEL, D_FF), D_MODEL ** -0.5),
        "ffn1_w_down": normal(ks[5], (DEPTH, D_FF, D_MODEL), D_FF ** -0.5),
        "ffn1_post_g": gain(ks[6], D_MODEL),
        "mix_pre_g": gain(ks[7], D_MODEL),
        "w_in": normal(ks[8], (DEPTH, D_MODEL, IN_COLS), D_MODEL ** -0.5),
        "mla_q_norm_g": gain(ks[9], Q_LORA),
        "mla_w_uq": normal(ks[10], (DEPTH, Q_LORA, MLA_HEADS * MLA_QK), Q_LORA ** -0.5),
        "mla_kv_norm_g": gain(ks[11], KV_LORA),
        "mla_w_ukv": normal(ks[12], (DEPTH, KV_LORA, MLA_HEADS * (MLA_NOPE + MLA_V)), KV_LORA ** -0.5),
        "mla_out_norm_g": gain(ks[13], MLA_OUT),
        "gdn_conv_w": normal(ks[14], (DEPTH, CONV_K, GDN_CONV_CH), CONV_K ** -0.5),
        "gdn_a_log": jnp.log(jax.random.uniform(ks[15], (DEPTH, 2, GDN_HEADS), f32, 1.0, 16.0)),
        "gdn_dt_bias": dt + jnp.log(-jnp.expm1(-dt)),
        "gdn_out_norm_g": gain(ks[16], GDN_DV),
        "w_out": normal(ks[17], (DEPTH, MIX_WIDTH, D_MODEL), MIX_WIDTH ** -0.5),
        "mix_post_g": gain(ks[18], D_MODEL),
        "ffn2_pre_g": gain(ks[19], D_MODEL),
        "ffn2_w_gate": normal(ks[21], (DEPTH, D_MODEL, D_FF), D_MODEL ** -0.5),
        "ffn2_w_up": normal(ks[22], (DEPTH, D_MODEL, D_FF), D_MODEL ** -0.5),
        "ffn2_w_down": normal(ks[23], (DEPTH, D_FF, D_MODEL), D_FF ** -0.5),
        "ffn2_post_g": gain(ks[24], D_MODEL),
        "final_norm_g": gain(ks[25], D_MODEL),
    }


def reference(x_prompt, x_sample, ffn1_pre_g, ffn1_w_gate, ffn1_w_up, ffn1_w_down, ffn1_post_g,
              mix_pre_g, w_in, mla_q_norm_g, mla_w_uq, mla_kv_norm_g, mla_w_ukv, mla_out_norm_g,
              gdn_conv_w, gdn_a_log, gdn_dt_bias, gdn_out_norm_g, w_out, mix_post_g,
              ffn2_pre_g, ffn2_w_gate, ffn2_w_up, ffn2_w_down, ffn2_post_g, final_norm_g):
    weights = (ffn1_pre_g, ffn1_w_gate, ffn1_w_up, ffn1_w_down, ffn1_post_g,
               mix_pre_g, w_in, mla_q_norm_g, mla_w_uq, mla_kv_norm_g, mla_w_ukv, mla_out_norm_g,
               gdn_conv_w, gdn_a_log, gdn_dt_bias, gdn_out_norm_g, w_out, mix_post_g,
               ffn2_pre_g, ffn2_w_gate, ffn2_w_up, ffn2_w_down, ffn2_post_g, final_norm_g)

    def trunk(x):
        h = x
        for l in range(DEPTH):
            h = encoder_layer(h, *[w[l] for w in weights])
        return h

    y_prompt = trunk(x_prompt)
    y_sample = trunk(x_sample)
    return (y_prompt, y_sample)
```

```python
import functools
import math

import jax
import jax.numpy as jnp
from jax import lax
from jax.experimental import pallas as pl
from jax.experimental.pallas import tpu as pltpu

F32 = jnp.float32
BF16 = jnp.bfloat16

EPS = 1e-6
MLA_HEADS = 8
MLA_NOPE = 64
MLA_ROPE = 32
MLA_QK = MLA_NOPE + MLA_ROPE
MLA_V = 64
HEAD_PAD = 128
V_ROWS = 80
ROPE_THETA = 10000.0
GDN_HEADS = 4
GDN_D = 128
CONV_K = 5
CHUNK = 64
LANE = 128

VMEM_LIMIT = 56 * 1024 * 1024


def _dot(a, b):
    return jnp.dot(a, b, preferred_element_type=F32)


def _dot_nt(a, b):
    return lax.dot_general(a, b, (((1,), (1,)), ((), ())), preferred_element_type=F32)


def _rms(x, g):
    ms = jnp.mean(x * x, axis=-1, keepdims=True)
    return x * lax.rsqrt(ms + EPS) * g


def _silu(x):
    return x * jax.nn.sigmoid(x)


def _const_spec(shape):
    nd = len(shape)
    return pl.BlockSpec(shape, lambda *_: (0,) * nd, pipeline_mode=pl.Buffered(1))


def _params(sem):
    return pltpu.CompilerParams(dimension_semantics=sem, vmem_limit_bytes=VMEM_LIMIT)


def _ffn_kernel(x_ref, pre_ref, wg_ref, wu_ref, wd_ref, post_ref, fin_ref, o_ref, *, final):
    x = x_ref[...]
    xn = _rms(x, pre_ref[...]).astype(BF16)
    g = _dot(xn, wg_ref[...])
    u = _dot(xn, wu_ref[...])
    a = (_silu(g) * u).astype(BF16)
    d = _dot(a, wd_ref[...])
    h = x + 0.5 * _rms(d, post_ref[...])
    if final:
        h = _rms(h, fin_ref[...])
    o_ref[...] = h


def _ffn(x2d, pre_g, wg, wu, wd, post_g, fin_g, final, tm):
    T, D = x2d.shape
    F = wg.shape[1]
    row = pl.BlockSpec((tm, D), lambda i: (i, 0))
    return pl.pallas_call(
        functools.partial(_ffn_kernel, final=final),
        grid=(T // tm,),
        in_specs=[row, _const_spec((1, D)), _const_spec((D, F)), _const_spec((D, F)),
                  _const_spec((F, D)), _const_spec((1, D)), _const_spec((1, D))],
        out_specs=row,
        out_shape=jax.ShapeDtypeStruct((T, D), F32),
        compiler_params=_params(("parallel",)),
        name="ffn_final" if final else "ffn",
    )(x2d, pre_g, wg, wu, wd, post_g, fin_g)


def _proj_kernel(h_ref, pre_ref, win_ref, gq_ref, wqa_ref, wqb_ref, gkv_ref, wk_ref, wv_ref,
                 cq_ref, sq_ref, ck_ref, sk_ref,
                 qt_ref, k_ref, vt_ref, qkv_ref, z_ref, ab_ref, *, splits):
    s_cq, s_ckv, s_kr, s_krr, s_qkv, s_z = splits
    u = _rms(h_ref[0], pre_ref[...]).astype(BF16)
    proj = _dot(u, win_ref[...])
    qkv_ref[0] = proj[:, s_krr:s_qkv]
    z_ref[0] = proj[:, s_qkv:s_z]
    ab_ref[0] = proj[:, s_z:]

    cqn = _rms(proj[:, :s_cq], gq_ref[...]).astype(BF16)
    qa = _dot(cqn, wqa_ref[...])
    qb = _dot(cqn, wqb_ref[...])
    cq, sq = cq_ref[...], sq_ref[...]
    q = jnp.concatenate(
        [qa[:, h * LANE:(h + 1) * LANE] * cq + qb[:, h * LANE:(h + 1) * LANE] * sq
         for h in range(MLA_HEADS)], axis=1)
    qt_ref[0] = q.T.astype(BF16)

    ckvn = _rms(proj[:, s_cq:s_ckv], gkv_ref[...]).astype(BF16)
    kpe = proj[:, s_ckv:s_kr] * ck_ref[...] + proj[:, s_kr:s_krr] * sk_ref[...]
    kb = _dot(ckvn, wk_ref[...])
    k_ref[0] = jnp.concatenate(
        [kb[:, h * LANE:(h + 1) * LANE] + kpe for h in range(MLA_HEADS)], axis=1).astype(BF16)

    vt = _dot(ckvn, wv_ref[...]).T.astype(BF16)
    tm = vt.shape[1]
    ones = jnp.ones((V_ROWS - MLA_V, tm), BF16)
    for h in range(MLA_HEADS):
        vt_ref[0, h * V_ROWS:h * V_ROWS + MLA_V, :] = vt[h * MLA_V:(h + 1) * MLA_V, :]
        vt_ref[0, h * V_ROWS + MLA_V:(h + 1) * V_ROWS, :] = ones


def _proj(h, w, tm):
    B, L, D = h.shape
    ncol = w["w_in"].shape[1]
    HP = MLA_HEADS * HEAD_PAD
    splits = w["splits"]
    n_qkv = splits[4] - splits[3]
    n_z = splits[5] - splits[4]
    tok = lambda n: pl.BlockSpec((1, tm, n), lambda b, i: (b, i, 0))
    tokT = lambda n: pl.BlockSpec((1, n, tm), lambda b, i: (b, 0, i))
    tab = pl.BlockSpec((tm, LANE), lambda b, i: (i, 0))
    cq_lora = w["wq_a"].shape[0]
    ckv_lora = w["wk"].shape[0]
    return pl.pallas_call(
        functools.partial(_proj_kernel, splits=splits),
        grid=(B, L // tm),
        in_specs=[tok(D), _const_spec((1, D)), _const_spec((D, ncol)),
                  _const_spec((1, cq_lora)), _const_spec((cq_lora, HP)), _const_spec((cq_lora, HP)),
                  _const_spec((1, ckv_lora)), _const_spec((ckv_lora, HP)),
                  _const_spec((ckv_lora, MLA_HEADS * MLA_V)),
                  tab, tab, tab, tab],
        out_specs=[tokT(HP), tok(HP), tokT(MLA_HEADS * V_ROWS), tok(n_qkv), tok(n_z), tok(LANE)],
        out_shape=[jax.ShapeDtypeStruct((B, HP, L), BF16),
                   jax.ShapeDtypeStruct((B, L, HP), BF16),
                   jax.ShapeDtypeStruct((B, MLA_HEADS * V_ROWS, L), BF16),
                   jax.ShapeDtypeStruct((B, L, n_qkv), F32),
                   jax.ShapeDtypeStruct((B, L, n_z), F32),
                   jax.ShapeDtypeStruct((B, L, LANE), F32)],
        compiler_params=_params(("parallel", "parallel")),
        name="proj",
    )(h, w["mix_pre_g"], w["w_in"], w["q_norm_g"], w["wq_a"], w["wq_b"], w["kv_norm_g"], w["wk"], w["wv"],
      w["cos_q"], w["sin_q"], w["cos_k"], w["sin_k"])


def _attn_kernel(qt_ref, k_ref, vt_ref, o_ref, *, tk):
    qt = qt_ref[0]
    tq = qt.shape[1]
    nk = k_ref.shape[1] // tk

    def body(j, carry):
        m, acc = carry
        off = pl.multiple_of(j * tk, tk)
        s = _dot(k_ref[0, pl.ds(off, tk), :], qt)
        m_new = jnp.maximum(m, jnp.max(s, axis=0, keepdims=True))
        p = jnp.exp2(s - m_new).astype(BF16)
        acc = acc * jnp.exp2(m - m_new) + _dot(vt_ref[0, :, pl.ds(off, tk)], p)
        return m_new, acc

    m0 = jnp.full((1, tq), -1e30, F32)
    acc0 = jnp.zeros((V_ROWS, tq), F32)
    _, acc = lax.fori_loop(0, nk, body, (m0, acc0))
    o_ref[0] = acc[:MLA_V, :] / acc[MLA_V:MLA_V + 1, :]


def _attn(qt, k, vt, tq, tk):
    B, HP, L = qt.shape
    return pl.pallas_call(
        functools.partial(_attn_kernel, tk=tk),
        grid=(B, MLA_HEADS, L // tq),
        in_specs=[pl.BlockSpec((1, HEAD_PAD, tq), lambda b, h, i: (b, h, i)),
                  pl.BlockSpec((1, L, HEAD_PAD), lambda b, h, i: (b, 0, h)),
                  pl.BlockSpec((1, V_ROWS, L), lambda b, h, i: (b, h, 0))],
        out_specs=pl.BlockSpec((1, MLA_V, tq), lambda b, h, i: (b, h, i)),
        out_shape=jax.ShapeDtypeStruct((B, MLA_HEADS * MLA_V, L), F32),
        compiler_params=_params(("parallel", "parallel", "parallel")),
        name="attn",
    )(qt, k, vt)


def _gdn_prep_kernel(x_ref, xp_ref, xn_ref, ab_ref, cw_ref, alog_ref, dtb_ref,
                     q_ref, k_ref, v_ref, g_ref):
    i = pl.program_id(1)
    n = pl.num_programs(1)
    x = x_ref[0]
    tm = x.shape[0]
    prev = jnp.where(i == 0, 0.0, xp_ref[0])
    nxt = jnp.where(i == n - 1, 0.0, xn_ref[0])
    ext = jnp.concatenate([prev, x, nxt], axis=0)
    half = CONV_K // 2
    acc = None
    for j in range(CONV_K):
        sh = (half - j) % (tm + 16)
        xs = ext if sh == 0 else pltpu.roll(ext, sh, 0)
        term = xs[8:8 + tm, :] * cw_ref[j:j + 1, :]
        acc = term if acc is None else acc + term
    y = _silu(acc)
    nqk = GDN_HEADS * GDN_D
    for h in range(GDN_HEADS):
        qh = y[:, h * GDN_D:(h + 1) * GDN_D]
        kh = y[:, nqk + h * GDN_D:nqk + (h + 1) * GDN_D]
        qn = qh * lax.rsqrt(jnp.sum(qh * qh, axis=-1, keepdims=True) + EPS) * (GDN_D ** -0.5)
        kn = kh * lax.rsqrt(jnp.sum(kh * kh, axis=-1, keepdims=True) + EPS)
        q_ref[0, :, h * GDN_D:(h + 1) * GDN_D] = qn
        k_ref[0, :, h * GDN_D:(h + 1) * GDN_D] = kn
    v_ref[0] = y[:, 2 * nqk:]
    ab = ab_ref[0]
    gate = -jnp.exp(alog_ref[...]) * jax.nn.softplus(ab + dtb_ref[...])
    beta = jax.nn.sigmoid(ab)
    lane = lax.broadcasted_iota(jnp.int32, ab.shape, 1)
    g_ref[0] = jnp.where(lane < 2 * GDN_HEADS, gate, beta)


def _gdn_prep(qkv, ab, conv_w, alog, dtb, tm):
    B, L, C = qkv.shape
    n8 = L // 8
    t8 = tm // 8
    nd = GDN_HEADS * GDN_D
    tok = lambda n: pl.BlockSpec((1, tm, n), lambda b, i: (b, i, 0))
    return pl.pallas_call(
        _gdn_prep_kernel,
        grid=(B, L // tm),
        in_specs=[tok(C),
                  pl.BlockSpec((1, 8, C), lambda b, i: (b, jnp.maximum(i * t8 - 1, 0), 0)),
                  pl.BlockSpec((1, 8, C), lambda b, i: (b, jnp.minimum((i + 1) * t8, n8 - 1), 0)),
                  tok(LANE), _const_spec((8, C)), _const_spec((1, LANE)), _const_spec((1, LANE))],
        out_specs=[tok(nd), tok(nd), tok(nd), tok(LANE)],
        out_shape=[jax.ShapeDtypeStruct((B, L, nd), F32)] * 3 + [jax.ShapeDtypeStruct((B, L, LANE), F32)],
        compiler_params=_params(("parallel", "parallel")),
        name="gdn_prep",
    )(qkv, qkv, qkv, ab, conv_w, alog, dtb)


def _split(x):
    hi = x.astype(BF16)
    lo = (x - hi.astype(F32)).astype(BF16)
    return hi, lo


def _tri_inv(a, eye, level_masks):
    d = eye - jnp.where(level_masks[0], a, 0.0)
    for m in level_masks[1:]:
        e = jnp.where(m, a, 0.0).astype(BF16)
        db = d.astype(BF16)
        d = d - _dot(db, _dot(e, db).astype(BF16))
    mh, ml = _split(eye + a)
    dh, dl = _split(d)
    r = eye - (_dot(mh, dh) + (_dot(mh, dl) + _dot(ml, dh)))
    return d + _dot(dh, r.astype(BF16))


def _gdn_kernel(q_ref, k_ref, v_ref, g_ref, o_ref, s_ref, *, rev, cg):
    @pl.when(pl.program_id(1) == 0)
    def _():
        s_ref[...] = jnp.zeros_like(s_ref)

    C = CHUNK
    row = lax.broadcasted_iota(jnp.int32, (C, C), 0)
    col = lax.broadcasted_iota(jnp.int32, (C, C), 1)
    incl = (row <= col) if rev else (row >= col)
    strict = (row < col) if rev else (row > col)
    eye = (row == col).astype(F32)
    tri = incl.astype(BF16)
    level_masks = []
    s = 1
    while s < C:
        level_masks.append(((row // (2 * s)) == (col // (2 * s))) & ((row // s) != (col // s)))
        s *= 2
    d = 1 if rev else 0
    last = 0 if rev else C - 1

    for c in (reversed(range(cg)) if rev else range(cg)):
        rows = slice(c * C, (c + 1) * C)
        gb = g_ref[0, rows, :]
        hi = gb.astype(BF16)
        r1 = gb - hi.astype(F32)
        mid = r1.astype(BF16)
        lo = (r1 - mid.astype(F32)).astype(BF16)
        gc_all = _dot(tri, hi) + (_dot(tri, mid) + _dot(tri, lo))
        gc_t = gc_all.T
        for h in range(GDN_HEADS):
            cgate = d * GDN_HEADS + h
            cbeta = 2 * GDN_HEADS + cgate
            hs = slice(h * GDN_D, (h + 1) * GDN_D)
            q = q_ref[0, rows, hs]
            k = k_ref[0, rows, hs]
            v = v_ref[0, rows, hs]
            gc = gc_all[:, cgate:cgate + 1]
            gr = gc_t[cgate:cgate + 1, :]
            beta = gb[:, cbeta:cbeta + 1]
            glast = gc[last:last + 1, :]
            decay = jnp.where(incl, jnp.exp(jnp.where(incl, gc - gr, 0.0)), 0.0)
            kb16 = k.astype(BF16)
            kk = _dot_nt(kb16, kb16)
            a = jnp.where(strict, kk * decay * beta, 0.0)
            t = _tri_inv(a, eye, level_masks).astype(BF16)
            eg = jnp.exp(gc)
            kv = jnp.concatenate([k * (beta * eg), v * beta], axis=1).astype(BF16)
            wu = _dot(t, kv)
            qk = jnp.where(incl, _dot_nt(q.astype(BF16), kb16) * decay, 0.0).astype(BF16)
            st = s_ref[h]
            wq = jnp.concatenate([wu[:, :GDN_D], q * eg], axis=0).astype(BF16)
            ws_qs = _dot(wq, st.astype(BF16))
            v_new = wu[:, GDN_D:] - ws_qs[:C]
            vn16 = v_new.astype(BF16)
            o_ref[0, rows, hs] = ws_qs[C:] + _dot(qk, vn16)
            kg_t = (k * jnp.exp(glast - gc)).T.astype(BF16)
            s_ref[h] = st * jnp.exp(glast) + _dot(kg_t, vn16)


def _gdn_scan(q, k, v, g, rev, cg):
    B, L, nd = q.shape
    tm = cg * CHUNK
    n = L // tm
    imap = (lambda b, i: (b, n - 1 - i, 0)) if rev else (lambda b, i: (b, i, 0))
    tok = lambda w: pl.BlockSpec((1, tm, w), imap)
    return pl.pallas_call(
        functools.partial(_gdn_kernel, rev=rev, cg=cg),
        grid=(B, n),
        in_specs=[tok(nd), tok(nd), tok(nd), tok(LANE)],
        out_specs=tok(nd),
        out_shape=jax.ShapeDtypeStruct((B, L, nd), F32),
        scratch_shapes=[pltpu.VMEM((GDN_HEADS, GDN_D, GDN_D), F32)],
        compiler_params=_params(("parallel", "arbitrary")),
        name="gdn_bwd" if rev else "gdn_fwd",
    )(q, k, v, g)


def _mix_kernel(h_ref, ot_ref, of_ref, ob_ref, z_ref, ga_ref, gg_ref, wa_ref, wb_ref, gp_ref, out_ref):
    ya = _rms(ot_ref[0].T, ga_ref[...]).astype(BF16)
    o = of_ref[0] + ob_ref[0]
    z = z_ref[0]
    gg = gg_ref[...]
    yb = jnp.concatenate(
        [_rms(o[:, h * GDN_D:(h + 1) * GDN_D], gg) * _silu(z[:, h * GDN_D:(h + 1) * GDN_D])
         for h in range(GDN_HEADS)], axis=1).astype(BF16)
    mix = _dot(ya, wa_ref[...]) + _dot(yb, wb_ref[...])
    out_ref[0] = h_ref[0] + _rms(mix, gp_ref[...])


def _mix(h, ot, of, ob, z, w, tm):
    B, L, D = h.shape
    na = ot.shape[1]
    nb = of.shape[2]
    tok = lambda n: pl.BlockSpec((1, tm, n), lambda b, i: (b, i, 0))
    return pl.pallas_call(
        _mix_kernel,
        grid=(B, L // tm),
        in_specs=[tok(D), pl.BlockSpec((1, na, tm), lambda b, i: (b, 0, i)), tok(nb), tok(nb), tok(nb),
                  _const_spec((1, na)), _const_spec((1, GDN_D)), _const_spec((na, D)), _const_spec((nb, D)),
                  _const_spec((1, D))],
        out_specs=tok(D),
        out_shape=jax.ShapeDtypeStruct((B, L, D), F32),
        compiler_params=_params(("parallel", "parallel")),
        name="mix",
    )(h, ot, of, ob, z, w["mla_out_g"], w["gdn_out_g"], w["wo_a"], w["wo_b"], w["mix_post_g"])


def _pad_lanes(a, left, width):
    return jnp.pad(a, ((0, 0), (left, width - left - a.shape[1])))


def _rot_half_cols(wmat):
    half = wmat.shape[-1] // 2
    return jnp.concatenate([-wmat[..., half:], wmat[..., :half]], axis=-1)


def _layer_weights(l, L, ffn1_pre_g, ffn1_w_gate, ffn1_w_up, ffn1_w_down, ffn1_post_g, mix_pre_g, w_in,
                   mla_q_norm_g, mla_w_uq, mla_kv_norm_g, mla_w_ukv, mla_out_norm_g, gdn_conv_w, gdn_a_log,
                   gdn_dt_bias, gdn_out_norm_g, w_out, mix_post_g, ffn2_pre_g, ffn2_w_gate, ffn2_w_up,
                   ffn2_w_down, ffn2_post_g, final_norm_g):
    row = lambda g: g[l][None, :].astype(F32)
    q_lora = mla_w_uq.shape[1]
    kv_lora = mla_w_ukv.shape[1]
    conv_ch = gdn_conv_w.shape[2]
    nz = GDN_HEADS * GDN_D
    c0 = q_lora
    c1 = c0 + kv_lora
    c2 = c1 + MLA_ROPE
    c3 = c2 + conv_ch
    c4 = c3 + nz
    wi = w_in[l]
    w_kr = wi[:, c1:c2]
    w_in_pad = jnp.concatenate([
        wi[:, :c1],
        _pad_lanes(w_kr, MLA_NOPE, LANE),
        _pad_lanes(_rot_half_cols(w_kr), MLA_NOPE, LANE),
        wi[:, c2:c4],
        _pad_lanes(wi[:, c4:], 0, LANE)], axis=1).astype(BF16)
    s_krr = c1 + 2 * LANE
    splits = (c0, c1, c1 + LANE, s_krr, s_krr + conv_ch, s_krr + conv_ch + nz)

    uq = mla_w_uq[l].reshape(q_lora, MLA_HEADS, MLA_QK)
    wq_a = jnp.pad(uq, ((0, 0), (0, 0), (0, HEAD_PAD - MLA_QK))).reshape(q_lora, -1).astype(BF16)
    wq_b = jnp.pad(_rot_half_cols(uq[..., MLA_NOPE:]),
                   ((0, 0), (0, 0), (MLA_NOPE, HEAD_PAD - MLA_QK))).reshape(q_lora, -1).astype(BF16)
    ukv = mla_w_ukv[l].reshape(kv_lora, MLA_HEADS, MLA_NOPE + MLA_V)
    wk = jnp.pad(ukv[..., :MLA_NOPE], ((0, 0), (0, 0), (0, HEAD_PAD - MLA_NOPE))).reshape(kv_lora, -1).astype(BF16)
    wv = ukv[..., MLA_NOPE:].reshape(kv_lora, -1).astype(BF16)

    inv = ROPE_THETA ** (-jnp.arange(0, MLA_ROPE, 2, dtype=F32) / MLA_ROPE)
    ang = jnp.arange(L, dtype=F32)[:, None] * inv[None, :]
    cos, sin = jnp.cos(ang), jnp.sin(ang)
    cos_k = jnp.concatenate([jnp.ones((L, MLA_NOPE), F32), cos, cos,
                             jnp.zeros((L, HEAD_PAD - MLA_QK), F32)], axis=1)
    sin_k = _pad_lanes(jnp.concatenate([sin, sin], axis=1), MLA_NOPE, LANE)
    qscale = (MLA_QK ** -0.5) * math.log2(math.e)

    alog = _pad_lanes(gdn_a_log[l].reshape(1, -1).astype(F32), 0, LANE)
    dtb = _pad_lanes(gdn_dt_bias[l].reshape(1, -1).astype(F32), 0, LANE)
    n_a = mla_out_norm_g.shape[1]
    return dict(
        ffn1=(row(ffn1_pre_g), ffn1_w_gate[l].astype(BF16), ffn1_w_up[l].astype(BF16),
              ffn1_w_down[l].astype(BF16), row(ffn1_post_g)),
        ffn2=(row(ffn2_pre_g), ffn2_w_gate[l].astype(BF16), ffn2_w_up[l].astype(BF16),
              ffn2_w_down[l].astype(BF16), row(ffn2_post_g)),
        final_g=row(final_norm_g),
        mix_pre_g=row(mix_pre_g), w_in=w_in_pad, splits=splits,
        q_norm_g=row(mla_q_norm_g), wq_a=wq_a, wq_b=wq_b, kv_norm_g=row(mla_kv_norm_g), wk=wk, wv=wv,
        cos_q=cos_k * qscale, sin_q=sin_k * qscale, cos_k=cos_k, sin_k=sin_k,
        conv_w=jnp.pad(gdn_conv_w[l].astype(F32), ((0, 8 - CONV_K), (0, 0))), alog=alog, dtb=dtb,
        mla_out_g=row(mla_out_norm_g), gdn_out_g=row(gdn_out_norm_g),
        wo_a=w_out[l][:n_a].astype(BF16), wo_b=w_out[l][n_a:].astype(BF16), mix_post_g=row(mix_post_g),
    )


def _tile(L, pref):
    return pref if L % pref == 0 else L


def _layer(h, w, last):
    B, L, D = h.shape
    tm = _tile(L, 512)
    h = _ffn(h.reshape(B * L, D), *w["ffn1"], w["final_g"], False, tm).reshape(B, L, D)
    qt, k, vt, qkv, z, ab = _proj(h, w, tm)
    ot = _attn(qt, k, vt, _tile(L, 512), _tile(L, 512))
    gq, gk, gv, gg = _gdn_prep(qkv, ab, w["conv_w"], w["alog"], w["dtb"], tm)
    cg = 2
    of = _gdn_scan(gq, gk, gv, gg, False, cg)
    ob = _gdn_scan(gq, gk, gv, gg, True, cg)
    h = _mix(h, ot, of, ob, z, w, tm)
    h = _ffn(h.reshape(B * L, D), *w["ffn2"], w["final_g"], last, tm).reshape(B, L, D)
    return h


def _trunk(x, weights):
    depth = weights[0].shape[0]
    h = x
    for l in range(depth):
        h = _layer(h, _layer_weights(l, x.shape[1], *weights), l == depth - 1)
    return h


def kernel(x_prompt, x_sample, ffn1_pre_g, ffn1_w_gate, ffn1_w_up, ffn1_w_down, ffn1_post_g, mix_pre_g, w_in, mla_q_norm_g, mla_w_uq, mla_kv_norm_g, mla_w_ukv, mla_out_norm_g, gdn_conv_w, gdn_a_log, gdn_dt_bias, gdn_out_norm_g, w_out, mix_post_g, ffn2_pre_g, ffn2_w_gate, ffn2_w_up, ffn2_w_down, ffn2_post_g, final_norm_g):
    weights = (ffn1_pre_g, ffn1_w_gate, ffn1_w_up, ffn1_w_down, ffn1_post_g, mix_pre_g, w_in,
               mla_q_norm_g, mla_w_uq, mla_kv_norm_g, mla_w_ukv, mla_out_norm_g, gdn_conv_w, gdn_a_log,
               gdn_dt_bias, gdn_out_norm_g, w_out, mix_post_g, ffn2_pre_g, ffn2_w_gate, ffn2_w_up,
               ffn2_w_down, ffn2_post_g, final_norm_g)
    return (_trunk(x_prompt, weights), _trunk(x_sample, weights))
```

```python
import functools
import math

import jax
import jax.numpy as jnp
from jax import lax
from jax.experimental import pallas as pl
from jax.experimental.pallas import tpu as pltpu

F32 = jnp.float32
BF16 = jnp.bfloat16

EPS = 1e-6
MLA_HEADS = 8
MLA_NOPE = 64
MLA_ROPE = 32
MLA_QK = MLA_NOPE + MLA_ROPE
MLA_V = 64
HEAD_PAD = 128
V_ROWS = 80
ROPE_THETA = 10000.0
GDN_HEADS = 4
GDN_D = 128
CONV_K = 5
CHUNK = 64
LANE = 128

ATTN_TQ = 512
ATTN_TK = 256
ATTN_NB = 4

VMEM_LIMIT = 56 * 1024 * 1024


def _dot(a, b):
    return jnp.dot(a, b, preferred_element_type=F32)


def _dot_nt(a, b):
    return lax.dot_general(a, b, (((1,), (1,)), ((), ())), preferred_element_type=F32)


def _rms(x, g):
    ms = jnp.mean(x * x, axis=-1, keepdims=True)
    return x * lax.rsqrt(ms + EPS) * g


def _silu(x):
    return x * jax.nn.sigmoid(x)


def _const_spec(shape):
    nd = len(shape)
    return pl.BlockSpec(shape, lambda *_: (0,) * nd, pipeline_mode=pl.Buffered(1))


def _params(sem):
    return pltpu.CompilerParams(dimension_semantics=sem, vmem_limit_bytes=VMEM_LIMIT)


def _ffn_kernel(x_ref, pre_ref, wg_ref, wu_ref, wd_ref, post_ref, fin_ref, o_ref, *, final):
    x = x_ref[...]
    xn = _rms(x, pre_ref[...]).astype(BF16)
    g = _dot(xn, wg_ref[...])
    u = _dot(xn, wu_ref[...])
    a = (_silu(g) * u).astype(BF16)
    d = _dot(a, wd_ref[...])
    h = x + 0.5 * _rms(d, post_ref[...])
    if final:
        h = _rms(h, fin_ref[...])
    o_ref[...] = h


def _ffn(x2d, pre_g, wg, wu, wd, post_g, fin_g, final, tm):
    T, D = x2d.shape
    F = wg.shape[1]
    row = pl.BlockSpec((tm, D), lambda i: (i, 0))
    return pl.pallas_call(
        functools.partial(_ffn_kernel, final=final),
        grid=(T // tm,),
        in_specs=[row, _const_spec((1, D)), _const_spec((D, F)), _const_spec((D, F)),
                  _const_spec((F, D)), _const_spec((1, D)), _const_spec((1, D))],
        out_specs=row,
        out_shape=jax.ShapeDtypeStruct((T, D), F32),
        compiler_params=_params(("parallel",)),
        name="ffn_final" if final else "ffn",
    )(x2d, pre_g, wg, wu, wd, post_g, fin_g)


def _proj_kernel(h_ref, pre_ref, win_ref, gq_ref, wqa_ref, wqb_ref, gkv_ref, wk_ref, wv_ref,
                 cq_ref, sq_ref, ck_ref, sk_ref,
                 qt_ref, k_ref, vt_ref, qkv_ref, z_ref, ab_ref, *, splits):
    s_cq, s_ckv, s_kr, s_krr, s_qkv, s_z = splits
    u = _rms(h_ref[0], pre_ref[...]).astype(BF16)
    proj = _dot(u, win_ref[...])
    qkv_ref[0] = proj[:, s_krr:s_qkv]
    z_ref[0] = proj[:, s_qkv:s_z]
    ab_ref[0] = proj[:, s_z:]

    cqn = _rms(proj[:, :s_cq], gq_ref[...]).astype(BF16)
    qa = _dot(cqn, wqa_ref[...])
    qb = _dot(cqn, wqb_ref[...])
    cq, sq = cq_ref[...], sq_ref[...]
    q = jnp.concatenate(
        [qa[:, h * LANE:(h + 1) * LANE] * cq + qb[:, h * LANE:(h + 1) * LANE] * sq
         for h in range(MLA_HEADS)], axis=1)
    qt_ref[0] = q.T.astype(BF16)

    ckvn = _rms(proj[:, s_cq:s_ckv], gkv_ref[...]).astype(BF16)
    kpe = proj[:, s_ckv:s_kr] * ck_ref[...] + proj[:, s_kr:s_krr] * sk_ref[...]
    kb = _dot(ckvn, wk_ref[...])
    k_ref[0] = jnp.concatenate(
        [kb[:, h * LANE:(h + 1) * LANE] + kpe for h in range(MLA_HEADS)], axis=1).astype(BF16)

    vt = _dot(ckvn, wv_ref[...]).T.astype(BF16)
    tm = vt.shape[1]
    ones = jnp.ones((V_ROWS - MLA_V, tm), BF16)
    for h in range(MLA_HEADS):
        vt_ref[0, h * V_ROWS:h * V_ROWS + MLA_V, :] = vt[h * MLA_V:(h + 1) * MLA_V, :]
        vt_ref[0, h * V_ROWS + MLA_V:(h + 1) * V_ROWS, :] = ones


def _proj(h, w, tm):
    B, L, D = h.shape
    ncol = w["w_in"].shape[1]
    HP = MLA_HEADS * HEAD_PAD
    splits = w["splits"]
    n_qkv = splits[4] - splits[3]
    n_z = splits[5] - splits[4]
    tok = lambda n: pl.BlockSpec((1, tm, n), lambda b, i: (b, i, 0))
    tokT = lambda n: pl.BlockSpec((1, n, tm), lambda b, i: (b, 0, i))
    tab = pl.BlockSpec((tm, LANE), lambda b, i: (i, 0))
    cq_lora = w["wq_a"].shape[0]
    ckv_lora = w["wk"].shape[0]
    return pl.pallas_call(
        functools.partial(_proj_kernel, splits=splits),
        grid=(B, L // tm),
        in_specs=[tok(D), _const_spec((1, D)), _const_spec((D, ncol)),
                  _const_spec((1, cq_lora)), _const_spec((cq_lora, HP)), _const_spec((cq_lora, HP)),
                  _const_spec((1, ckv_lora)), _const_spec((ckv_lora, HP)),
                  _const_spec((ckv_lora, MLA_HEADS * MLA_V)),
                  tab, tab, tab, tab],
        out_specs=[tokT(HP), tok(HP), tokT(MLA_HEADS * V_ROWS), tok(n_qkv), tok(n_z), tok(LANE)],
        out_shape=[jax.ShapeDtypeStruct((B, HP, L), BF16),
                   jax.ShapeDtypeStruct((B, L, HP), BF16),
                   jax.ShapeDtypeStruct((B, MLA_HEADS * V_ROWS, L), BF16),
                   jax.ShapeDtypeStruct((B, L, n_qkv), F32),
                   jax.ShapeDtypeStruct((B, L, n_z), F32),
                   jax.ShapeDtypeStruct((B, L, LANE), F32)],
        compiler_params=_params(("parallel", "parallel")),
        name="proj",
    )(h, w["mix_pre_g"], w["w_in"], w["q_norm_g"], w["wq_a"], w["wq_b"], w["kv_norm_g"], w["wk"], w["wv"],
      w["cos_q"], w["sin_q"], w["cos_k"], w["sin_k"])


def _attn_kernel(qt_ref, k_ref, vt_ref, o_ref, *, tk, nb):
    qt = qt_ref[0]
    tq = qt.shape[1]
    step = tk * nb
    nk = k_ref.shape[1] // step

    def body(j, carry):
        m, acc = carry
        off = pl.multiple_of(j * step, step)
        ss = [_dot(k_ref[0, pl.ds(off + i * tk, tk), :], qt) for i in range(nb)]
        for i in range(nb):
            m_new = jnp.maximum(m, jnp.max(ss[i], axis=0, keepdims=True))
            p = jnp.exp2(ss[i] - m_new).astype(BF16)
            acc = acc * jnp.exp2(m - m_new) + _dot(vt_ref[0, :, pl.ds(off + i * tk, tk)], p)
            m = m_new
        return m, acc

    m0 = jnp.full((1, tq), -1e30, F32)
    acc0 = jnp.zeros((V_ROWS, tq), F32)
    _, acc = lax.fori_loop(0, nk, body, (m0, acc0))
    o_ref[0] = acc[:MLA_V, :] / acc[MLA_V:MLA_V + 1, :]


def _attn(qt, k, vt, tq, tk, nb):
    B, HP, L = qt.shape
    return pl.pallas_call(
        functools.partial(_attn_kernel, tk=tk, nb=nb),
        grid=(B, MLA_HEADS, L // tq),
        in_specs=[pl.BlockSpec((1, HEAD_PAD, tq), lambda b, h, i: (b, h, i)),
                  pl.BlockSpec((1, L, HEAD_PAD), lambda b, h, i: (b, 0, h)),
                  pl.BlockSpec((1, V_ROWS, L), lambda b, h, i: (b, h, 0))],
        out_specs=pl.BlockSpec((1, MLA_V, tq), lambda b, h, i: (b, h, i)),
        out_shape=jax.ShapeDtypeStruct((B, MLA_HEADS * MLA_V, L), F32),
        compiler_params=_params(("parallel", "parallel", "parallel")),
        name="attn",
    )(qt, k, vt)


def _gdn_prep_kernel(x_ref, xp_ref, xn_ref, ab_ref, cw_ref, alog_ref, dtb_ref,
                     q_ref, k_ref, v_ref, g_ref):
    i = pl.program_id(1)
    n = pl.num_programs(1)
    x = x_ref[0]
    tm = x.shape[0]
    prev = jnp.where(i == 0, 0.0, xp_ref[0])
    nxt = jnp.where(i == n - 1, 0.0, xn_ref[0])
    ext = jnp.concatenate([prev, x, nxt], axis=0)
    half = CONV_K // 2
    acc = None
    for j in range(CONV_K):
        sh = (half - j) % (tm + 16)
        xs = ext if sh == 0 else pltpu.roll(ext, sh, 0)
        term = xs[8:8 + tm, :] * cw_ref[j:j + 1, :]
        acc = term if acc is None else acc + term
    y = _silu(acc)
    nqk = GDN_HEADS * GDN_D
    for h in range(GDN_HEADS):
        qh = y[:, h * GDN_D:(h + 1) * GDN_D]
        kh = y[:, nqk + h * GDN_D:nqk + (h + 1) * GDN_D]
        qn = qh * lax.rsqrt(jnp.sum(qh * qh, axis=-1, keepdims=True) + EPS) * (GDN_D ** -0.5)
        kn = kh * lax.rsqrt(jnp.sum(kh * kh, axis=-1, keepdims=True) + EPS)
        q_ref[0, :, h * GDN_D:(h + 1) * GDN_D] = qn
        k_ref[0, :, h * GDN_D:(h + 1) * GDN_D] = kn
    v_ref[0] = y[:, 2 * nqk:]
    ab = ab_ref[0]
    gate = -jnp.exp(alog_ref[...]) * jax.nn.softplus(ab + dtb_ref[...])
    beta = jax.nn.sigmoid(ab)
    lane = lax.broadcasted_iota(jnp.int32, ab.shape, 1)
    g_ref[0] = jnp.where(lane < 2 * GDN_HEADS, gate, beta)


def _gdn_prep(qkv, ab, conv_w, alog, dtb, tm):
    B, L, C = qkv.shape
    n8 = L // 8
    t8 = tm // 8
    nd = GDN_HEADS * GDN_D
    tok = lambda n: pl.BlockSpec((1, tm, n), lambda b, i: (b, i, 0))
    return pl.pallas_call(
        _gdn_prep_kernel,
        grid=(B, L // tm),
        in_specs=[tok(C),
                  pl.BlockSpec((1, 8, C), lambda b, i: (b, jnp.maximum(i * t8 - 1, 0), 0)),
                  pl.BlockSpec((1, 8, C), lambda b, i: (b, jnp.minimum((i + 1) * t8, n8 - 1), 0)),
                  tok(LANE), _const_spec((8, C)), _const_spec((1, LANE)), _const_spec((1, LANE))],
        out_specs=[tok(nd), tok(nd), tok(nd), tok(LANE)],
        out_shape=[jax.ShapeDtypeStruct((B, L, nd), F32)] * 3 + [jax.ShapeDtypeStruct((B, L, LANE), F32)],
        compiler_params=_params(("parallel", "parallel")),
        name="gdn_prep",
    )(qkv, qkv, qkv, ab, conv_w, alog, dtb)


def _split(x):
    hi = x.astype(BF16)
    lo = (x - hi.astype(F32)).astype(BF16)
    return hi, lo


def _bmm(a, b):
    return lax.dot_general(a, b, (((2,), (1,)), ((0,), (0,))), preferred_element_type=F32)


def _bmm_nt(a, b):
    return lax.dot_general(a, b, (((2,), (2,)), ((0,), (0,))), preferred_element_type=F32)


def _tri_inv(a, eye, level_masks):
    d = eye - jnp.where(level_masks[0], a, 0.0)
    for m in level_masks[1:]:
        e = jnp.where(m, a, 0.0).astype(BF16)
        db = d.astype(BF16)
        d = d - _bmm(db, _bmm(e, db).astype(BF16))
    mh, ml = _split(eye + a)
    dh, dl = _split(d)
    r = eye - (_bmm(mh, dh) + (_bmm(mh, dl) + _bmm(ml, dh)))
    return d + _bmm(dh, r.astype(BF16))


def _gdn_kernel(qf_ref, kf_ref, vf_ref, gf_ref, qb_ref, kb_ref, vb_ref, gb_ref, of_ref, ob_ref, s_ref, *, cg):
    @pl.when(pl.program_id(1) == 0)
    def _():
        s_ref[...] = jnp.zeros_like(s_ref)

    C = CHUNK
    NH = GDN_HEADS
    refs = ((qf_ref, kf_ref, vf_ref, gf_ref, of_ref), (qb_ref, kb_ref, vb_ref, gb_ref, ob_ref))
    row = lax.broadcasted_iota(jnp.int32, (C, C), 0)
    col = lax.broadcasted_iota(jnp.int32, (C, C), 1)
    eye = (row == col).astype(F32)
    level_masks = []
    s = 1
    while s < C:
        level_masks.append(((row // (2 * s)) == (col // (2 * s))) & ((row // s) != (col // s)))
        s *= 2

    def dir_select(n, per_dir, when_bwd, when_fwd):
        idx = lax.broadcasted_iota(jnp.int32, (n, C, C), 0)
        is_bwd = (idx // per_dir) % 2 == 1
        return (is_bwd & when_bwd[None]) | (~is_bwd & when_fwd[None])

    def chunk_rows(j, d):
        c = cg - 1 - j if d else j
        return slice(c * C, (c + 1) * C)

    pairs = [(j, d) for j in range(cg) for d in range(2)]
    units = [(j, d, h) for (j, d) in pairs for h in range(NH)]
    g_all = jnp.stack([refs[d][3][0, chunk_rows(j, d), :] for (j, d) in pairs])
    hi = g_all.astype(BF16)
    r1 = g_all - hi.astype(F32)
    mid = r1.astype(BF16)
    lo = (r1 - mid.astype(F32)).astype(BF16)
    tri = dir_select(len(pairs), 1, row <= col, row >= col).astype(BF16)
    gc_all = _bmm(tri, hi) + (_bmm(tri, mid) + _bmm(tri, lo))
    gc_t = [gc_all[p].T for p in range(len(pairs))]

    def per_unit(fn):
        return jnp.stack([fn(j, d, h, j * 2 + d, d * NH + h) for (j, d, h) in units])

    hs = lambda h: slice(h * GDN_D, (h + 1) * GDN_D)
    q = per_unit(lambda j, d, h, p, cgate: refs[d][0][0, chunk_rows(j, d), hs(h)])
    k = per_unit(lambda j, d, h, p, cgate: refs[d][1][0, chunk_rows(j, d), hs(h)])
    v = per_unit(lambda j, d, h, p, cgate: refs[d][2][0, chunk_rows(j, d), hs(h)])
    gc = per_unit(lambda j, d, h, p, cgate: gc_all[p][:, cgate:cgate + 1])
    gr = per_unit(lambda j, d, h, p, cgate: gc_t[p][cgate:cgate + 1, :])
    beta = per_unit(lambda j, d, h, p, cgate: g_all[p][:, 2 * NH + cgate:2 * NH + cgate + 1])
    glast = per_unit(lambda j, d, h, p, cgate: gc_all[p][(0 if d else C - 1):(1 if d else C), cgate:cgate + 1])

    U = len(units)
    incl = dir_select(U, NH, row <= col, row >= col)
    strict = dir_select(U, NH, row < col, row > col)
    decay = jnp.where(incl, jnp.exp(jnp.where(incl, gc - gr, 0.0)), 0.0)
    k16 = k.astype(BF16)
    a = jnp.where(strict, _bmm_nt(k16, k16) * decay * beta, 0.0)
    t = _tri_inv(a, eye, level_masks).astype(BF16)
    eg = jnp.exp(gc)
    kv = jnp.concatenate([k * (beta * eg), v * beta], axis=2).astype(BF16)
    wu = _bmm(t, kv)
    qk = jnp.where(incl, _bmm_nt(q.astype(BF16), k16) * decay, 0.0).astype(BF16)
    wq = jnp.concatenate([wu[:, :, :GDN_D], q * eg], axis=1).astype(BF16)
    u_part = wu[:, :, GDN_D:]
    kg = k * jnp.exp(glast - gc)
    kg_t = jnp.stack([kg[i].T for i in range(U)]).astype(BF16)
    dlast = jnp.exp(glast)

    st = s_ref[...]
    n_chain = 2 * NH
    for j in range(cg):
        sl = slice(j * n_chain, (j + 1) * n_chain)
        ws_qs = _bmm(wq[sl], st.astype(BF16))
        v_new = u_part[sl] - ws_qs[:, :C]
        vn16 = v_new.astype(BF16)
        o = ws_qs[:, C:] + _bmm(qk[sl], vn16)
        st = st * dlast[sl] + _bmm(kg_t[sl], vn16)
        for d in range(2):
            for h in range(NH):
                refs[d][4][0, chunk_rows(j, d), hs(h)] = o[d * NH + h]
    s_ref[...] = st


def _gdn_scan(q, k, v, g, cg):
    B, L, nd = q.shape
    tm = cg * CHUNK
    n = L // tm
    fwd = lambda w: pl.BlockSpec((1, tm, w), lambda b, i: (b, i, 0))
    bwd = lambda w: pl.BlockSpec((1, tm, w), lambda b, i: (b, n - 1 - i, 0))
    out = jax.ShapeDtypeStruct((B, L, nd), F32)
    return pl.pallas_call(
        functools.partial(_gdn_kernel, cg=cg),
        grid=(B, n),
        in_specs=[fwd(nd), fwd(nd), fwd(nd), fwd(LANE), bwd(nd), bwd(nd), bwd(nd), bwd(LANE)],
        out_specs=[fwd(nd), bwd(nd)],
        out_shape=[out, out],
        scratch_shapes=[pltpu.VMEM((2 * GDN_HEADS, GDN_D, GDN_D), F32)],
        compiler_params=_params(("parallel", "arbitrary")),
        name="gdn_scan",
    )(q, k, v, g, q, k, v, g)


def _mix_kernel(h_ref, ot_ref, of_ref, ob_ref, z_ref, ga_ref, gg_ref, wa_ref, wb_ref, gp_ref, out_ref):
    ya = _rms(ot_ref[0].T, ga_ref[...]).astype(BF16)
    o = of_ref[0] + ob_ref[0]
    z = z_ref[0]
    gg = gg_ref[...]
    yb = jnp.concatenate(
        [_rms(o[:, h * GDN_D:(h + 1) * GDN_D], gg) * _silu(z[:, h * GDN_D:(h + 1) * GDN_D])
         for h in range(GDN_HEADS)], axis=1).astype(BF16)
    mix = _dot(ya, wa_ref[...]) + _dot(yb, wb_ref[...])
    out_ref[0] = h_ref[0] + _rms(mix, gp_ref[...])


def _mix(h, ot, of, ob, z, w, tm):
    B, L, D = h.shape
    na = ot.shape[1]
    nb = of.shape[2]
    tok = lambda n: pl.BlockSpec((1, tm, n), lambda b, i: (b, i, 0))
    return pl.pallas_call(
        _mix_kernel,
        grid=(B, L // tm),
        in_specs=[tok(D), pl.BlockSpec((1, na, tm), lambda b, i: (b, 0, i)), tok(nb), tok(nb), tok(nb),
                  _const_spec((1, na)), _const_spec((1, GDN_D)), _const_spec((na, D)), _const_spec((nb, D)),
                  _const_spec((1, D))],
        out_specs=tok(D),
        out_shape=jax.ShapeDtypeStruct((B, L, D), F32),
        compiler_params=_params(("parallel", "parallel")),
        name="mix",
    )(h, ot, of, ob, z, w["mla_out_g"], w["gdn_out_g"], w["wo_a"], w["wo_b"], w["mix_post_g"])


def _pad_lanes(a, left, width):
    return jnp.pad(a, ((0, 0), (left, width - left - a.shape[1])))


def _rot_half_cols(wmat):
    half = wmat.shape[-1] // 2
    return jnp.concatenate([-wmat[..., half:], wmat[..., :half]], axis=-1)


def _layer_weights(l, L, ffn1_pre_g, ffn1_w_gate, ffn1_w_up, ffn1_w_down, ffn1_post_g, mix_pre_g, w_in,
                   mla_q_norm_g, mla_w_uq, mla_kv_norm_g, mla_w_ukv, mla_out_norm_g, gdn_conv_w, gdn_a_log,
                   gdn_dt_bias, gdn_out_norm_g, w_out, mix_post_g, ffn2_pre_g, ffn2_w_gate, ffn2_w_up,
                   ffn2_w_down, ffn2_post_g, final_norm_g):
    row = lambda g: g[l][None, :].astype(F32)
    q_lora = mla_w_uq.shape[1]
    kv_lora = mla_w_ukv.shape[1]
    conv_ch = gdn_conv_w.shape[2]
    nz = GDN_HEADS * GDN_D
    c0 = q_lora
    c1 = c0 + kv_lora
    c2 = c1 + MLA_ROPE
    c3 = c2 + conv_ch
    c4 = c3 + nz
    wi = w_in[l]
    w_kr = wi[:, c1:c2]
    w_in_pad = jnp.concatenate([
        wi[:, :c1],
        _pad_lanes(w_kr, MLA_NOPE, LANE),
        _pad_lanes(_rot_half_cols(w_kr), MLA_NOPE, LANE),
        wi[:, c2:c4],
        _pad_lanes(wi[:, c4:], 0, LANE)], axis=1).astype(BF16)
    s_krr = c1 + 2 * LANE
    splits = (c0, c1, c1 + LANE, s_krr, s_krr + conv_ch, s_krr + conv_ch + nz)

    uq = mla_w_uq[l].reshape(q_lora, MLA_HEADS, MLA_QK)
    wq_a = jnp.pad(uq, ((0, 0), (0, 0), (0, HEAD_PAD - MLA_QK))).reshape(q_lora, -1).astype(BF16)
    wq_b = jnp.pad(_rot_half_cols(uq[..., MLA_NOPE:]),
                   ((0, 0), (0, 0), (MLA_NOPE, HEAD_PAD - MLA_QK))).reshape(q_lora, -1).astype(BF16)
    ukv = mla_w_ukv[l].reshape(kv_lora, MLA_HEADS, MLA_NOPE + MLA_V)
    wk = jnp.pad(ukv[..., :MLA_NOPE], ((0, 0), (0, 0), (0, HEAD_PAD - MLA_NOPE))).reshape(kv_lora, -1).astype(BF16)
    wv = ukv[..., MLA_NOPE:].reshape(kv_lora, -1).astype(BF16)

    inv = ROPE_THETA ** (-jnp.arange(0, MLA_ROPE, 2, dtype=F32) / MLA_ROPE)
    ang = jnp.arange(L, dtype=F32)[:, None] * inv[None, :]
    cos, sin = jnp.cos(ang), jnp.sin(ang)
    cos_k = jnp.concatenate([jnp.ones((L, MLA_NOPE), F32), cos, cos,
                             jnp.zeros((L, HEAD_PAD - MLA_QK), F32)], axis=1)
    sin_k = _pad_lanes(jnp.concatenate([sin, sin], axis=1), MLA_NOPE, LANE)
    qscale = (MLA_QK ** -0.5) * math.log2(math.e)

    alog = _pad_lanes(gdn_a_log[l].reshape(1, -1).astype(F32), 0, LANE)
    dtb = _pad_lanes(gdn_dt_bias[l].reshape(1, -1).astype(F32), 0, LANE)
    n_a = mla_out_norm_g.shape[1]
    return dict(
        ffn1=(row(ffn1_pre_g), ffn1_w_gate[l].astype(BF16), ffn1_w_up[l].astype(BF16),
              ffn1_w_down[l].astype(BF16), row(ffn1_post_g)),
        ffn2=(row(ffn2_pre_g), ffn2_w_gate[l].astype(BF16), ffn2_w_up[l].astype(BF16),
              ffn2_w_down[l].astype(BF16), row(ffn2_post_g)),
        final_g=row(final_norm_g),
        mix_pre_g=row(mix_pre_g), w_in=w_in_pad, splits=splits,
        q_norm_g=row(mla_q_norm_g), wq_a=wq_a, wq_b=wq_b, kv_norm_g=row(mla_kv_norm_g), wk=wk, wv=wv,
        cos_q=cos_k * qscale, sin_q=sin_k * qscale, cos_k=cos_k, sin_k=sin_k,
        conv_w=jnp.pad(gdn_conv_w[l].astype(F32), ((0, 8 - CONV_K), (0, 0))), alog=alog, dtb=dtb,
        mla_out_g=row(mla_out_norm_g), gdn_out_g=row(gdn_out_norm_g),
        wo_a=w_out[l][:n_a].astype(BF16), wo_b=w_out[l][n_a:].astype(BF16), mix_post_g=row(mix_post_g),
    )


def _tile(L, pref):
    return pref if L % pref == 0 else L


def _layer(h, w, last):
    B, L, D = h.shape
    tm = _tile(L, 512)
    h = _ffn(h.reshape(B * L, D), *w["ffn1"], w["final_g"], False, tm).reshape(B, L, D)
    qt, k, vt, qkv, z, ab = _proj(h, w, tm)
    tk = _tile(L, ATTN_TK)
    nb = ATTN_NB if L % (tk * ATTN_NB) == 0 else 1
    ot = _attn(qt, k, vt, _tile(L, ATTN_TQ), tk, nb)
    gq, gk, gv, gg = _gdn_prep(qkv, ab, w["conv_w"], w["alog"], w["dtb"], tm)
    of, ob = _gdn_scan(gq, gk, gv, gg, 2)
    h = _mix(h, ot, of, ob, z, w, tm)
    h = _ffn(h.reshape(B * L, D), *w["ffn2"], w["final_g"], last, tm).reshape(B, L, D)
    return h


def _trunk(x, weights):
    depth = weights[0].shape[0]
    h = x
    for l in range(depth):
        h = _layer(h, _layer_weights(l, x.shape[1], *weights), l == depth - 1)
    return h


def kernel(x_prompt, x_sample, ffn1_pre_g, ffn1_w_gate, ffn1_w_up, ffn1_w_down, ffn1_post_g, mix_pre_g, w_in, mla_q_norm_g, mla_w_uq, mla_kv_norm_g, mla_w_ukv, mla_out_norm_g, gdn_conv_w, gdn_a_log, gdn_dt_bias, gdn_out_norm_g, w_out, mix_post_g, ffn2_pre_g, ffn2_w_gate, ffn2_w_up, ffn2_w_down, ffn2_post_g, final_norm_g):
    weights = (ffn1_pre_g, ffn1_w_gate, ffn1_w_up, ffn1_w_down, ffn1_post_g, mix_pre_g, w_in,
               mla_q_norm_g, mla_w_uq, mla_kv_norm_g, mla_w_ukv, mla_out_norm_g, gdn_conv_w, gdn_a_log,
               gdn_dt_bias, gdn_out_norm_g, w_out, mix_post_g, ffn2_pre_g, ffn2_w_gate, ffn2_w_up,
               ffn2_w_down, ffn2_post_g, final_norm_g)
    return (_trunk(x_prompt, weights), _trunk(x_sample, weights))
```

```python
import functools
import math

import jax
import jax.numpy as jnp
from jax import lax
from jax.experimental import pallas as pl
from jax.experimental.pallas import tpu as pltpu

F32 = jnp.float32
BF16 = jnp.bfloat16

EPS = 1e-6
MLA_HEADS = 8
MLA_NOPE = 64
MLA_ROPE = 32
MLA_QK = MLA_NOPE + MLA_ROPE
MLA_V = 64
HEAD_PAD = 128
V_ROWS = 80
ROPE_THETA = 10000.0
GDN_HEADS = 4
GDN_D = 128
CONV_K = 5
CHUNK = 64
LANE = 128

ATTN_TQ = 1024
ATTN_TK = 256
ATTN_NB = 16
ATTN_AHEAD = 2
GDN_CG = 4

VMEM_LIMIT = 56 * 1024 * 1024


def _dot(a, b):
    return jnp.dot(a, b, preferred_element_type=F32)


def _dot_nt(a, b):
    return lax.dot_general(a, b, (((1,), (1,)), ((), ())), preferred_element_type=F32)


def _rms(x, g):
    ms = jnp.mean(x * x, axis=-1, keepdims=True)
    return x * lax.rsqrt(ms + EPS) * g


def _silu(x):
    return x * jax.nn.sigmoid(x)


def _const_spec(shape):
    nd = len(shape)
    return pl.BlockSpec(shape, lambda *_: (0,) * nd, pipeline_mode=pl.Buffered(1))


def _params(sem):
    return pltpu.CompilerParams(dimension_semantics=sem, vmem_limit_bytes=VMEM_LIMIT)


def _ffn_kernel(x_ref, pre_ref, wg_ref, wu_ref, wd_ref, post_ref, fin_ref, o_ref, *, final):
    x = x_ref[...]
    xn = _rms(x, pre_ref[...]).astype(BF16)
    g = _dot(xn, wg_ref[...])
    u = _dot(xn, wu_ref[...])
    a = (_silu(g) * u).astype(BF16)
    d = _dot(a, wd_ref[...])
    h = x + 0.5 * _rms(d, post_ref[...])
    if final:
        h = _rms(h, fin_ref[...])
    o_ref[...] = h


def _ffn(x2d, pre_g, wg, wu, wd, post_g, fin_g, final, tm):
    T, D = x2d.shape
    F = wg.shape[1]
    row = pl.BlockSpec((tm, D), lambda i: (i, 0))
    return pl.pallas_call(
        functools.partial(_ffn_kernel, final=final),
        grid=(T // tm,),
        in_specs=[row, _const_spec((1, D)), _const_spec((D, F)), _const_spec((D, F)),
                  _const_spec((F, D)), _const_spec((1, D)), _const_spec((1, D))],
        out_specs=row,
        out_shape=jax.ShapeDtypeStruct((T, D), F32),
        compiler_params=_params(("parallel",)),
        name="ffn_final" if final else "ffn",
    )(x2d, pre_g, wg, wu, wd, post_g, fin_g)


def _proj_kernel(h_ref, pre_ref, win_ref, gq_ref, wqa_ref, wqb_ref, gkv_ref, wk_ref, wv_ref,
                 cq_ref, sq_ref, ck_ref, sk_ref,
                 qt_ref, k_ref, vt_ref, qkv_ref, z_ref, ab_ref, *, splits):
    s_cq, s_ckv, s_kr, s_krr, s_qkv, s_z = splits
    u = _rms(h_ref[0], pre_ref[...]).astype(BF16)
    proj = _dot(u, win_ref[...])
    qkv_ref[0] = proj[:, s_krr:s_qkv]
    z_ref[0] = proj[:, s_qkv:s_z]
    ab_ref[0] = proj[:, s_z:]

    cqn = _rms(proj[:, :s_cq], gq_ref[...]).astype(BF16)
    qa = _dot(cqn, wqa_ref[...])
    qb = _dot(cqn, wqb_ref[...])
    cq, sq = cq_ref[...], sq_ref[...]
    q = jnp.concatenate(
        [qa[:, h * LANE:(h + 1) * LANE] * cq + qb[:, h * LANE:(h + 1) * LANE] * sq
         for h in range(MLA_HEADS)], axis=1)
    qt_ref[0] = q.T.astype(BF16)

    ckvn = _rms(proj[:, s_cq:s_ckv], gkv_ref[...]).astype(BF16)
    kpe = proj[:, s_ckv:s_kr] * ck_ref[...] + proj[:, s_kr:s_krr] * sk_ref[...]
    kb = _dot(ckvn, wk_ref[...])
    k_ref[0] = jnp.concatenate(
        [kb[:, h * LANE:(h + 1) * LANE] + kpe for h in range(MLA_HEADS)], axis=1).astype(BF16)

    vt = _dot(ckvn, wv_ref[...]).T.astype(BF16)
    tm = vt.shape[1]
    ones = jnp.ones((V_ROWS - MLA_V, tm), BF16)
    for h in range(MLA_HEADS):
        vt_ref[0, h * V_ROWS:h * V_ROWS + MLA_V, :] = vt[h * MLA_V:(h + 1) * MLA_V, :]
        vt_ref[0, h * V_ROWS + MLA_V:(h + 1) * V_ROWS, :] = ones


def _proj(h, w, tm):
    B, L, D = h.shape
    ncol = w["w_in"].shape[1]
    HP = MLA_HEADS * HEAD_PAD
    splits = w["splits"]
    n_qkv = splits[4] - splits[3]
    n_z = splits[5] - splits[4]
    tok = lambda n: pl.BlockSpec((1, tm, n), lambda b, i: (b, i, 0))
    tokT = lambda n: pl.BlockSpec((1, n, tm), lambda b, i: (b, 0, i))
    tab = pl.BlockSpec((tm, LANE), lambda b, i: (i, 0))
    cq_lora = w["wq_a"].shape[0]
    ckv_lora = w["wk"].shape[0]
    return pl.pallas_call(
        functools.partial(_proj_kernel, splits=splits),
        grid=(B, L // tm),
        in_specs=[tok(D), _const_spec((1, D)), _const_spec((D, ncol)),
                  _const_spec((1, cq_lora)), _const_spec((cq_lora, HP)), _const_spec((cq_lora, HP)),
                  _const_spec((1, ckv_lora)), _const_spec((ckv_lora, HP)),
                  _const_spec((ckv_lora, MLA_HEADS * MLA_V)),
                  tab, tab, tab, tab],
        out_specs=[tokT(HP), tok(HP), tokT(MLA_HEADS * V_ROWS), tok(n_qkv), tok(n_z), tok(LANE)],
        out_shape=[jax.ShapeDtypeStruct((B, HP, L), BF16),
                   jax.ShapeDtypeStruct((B, L, HP), BF16),
                   jax.ShapeDtypeStruct((B, MLA_HEADS * V_ROWS, L), BF16),
                   jax.ShapeDtypeStruct((B, L, n_qkv), F32),
                   jax.ShapeDtypeStruct((B, L, n_z), F32),
                   jax.ShapeDtypeStruct((B, L, LANE), F32)],
        compiler_params=_params(("parallel", "parallel")),
        name="proj",
    )(h, w["mix_pre_g"], w["w_in"], w["q_norm_g"], w["wq_a"], w["wq_b"], w["kv_norm_g"], w["wk"], w["wv"],
      w["cos_q"], w["sin_q"], w["cos_k"], w["sin_k"])


def _attn_kernel(qt_ref, k_ref, vt_ref, o_ref, *, tk, nb):
    qt = qt_ref[0]
    tq = qt.shape[1]
    step = tk * nb
    nk = k_ref.shape[1] // step

    def body(j, carry):
        m, acc = carry
        off = pl.multiple_of(j * step, step)
        scores = lambda i: _dot(k_ref[0, pl.ds(off + i * tk, tk), :], qt)
        ss = [scores(i) for i in range(min(ATTN_AHEAD, nb))]
        for i in range(nb):
            if i + ATTN_AHEAD < nb:
                ss.append(scores(i + ATTN_AHEAD))
            m_new = jnp.maximum(m, jnp.max(ss[i], axis=0, keepdims=True))
            p = jnp.exp2(ss[i] - m_new).astype(BF16)
            acc = acc * jnp.exp2(m - m_new) + _dot(vt_ref[0, :, pl.ds(off + i * tk, tk)], p)
            m = m_new
        return m, acc

    m0 = jnp.full((1, tq), -1e30, F32)
    acc0 = jnp.zeros((V_ROWS, tq), F32)
    _, acc = lax.fori_loop(0, nk, body, (m0, acc0))
    o_ref[0] = acc[:MLA_V, :] / acc[MLA_V:MLA_V + 1, :]


def _attn(qt, k, vt, tq, tk, nb):
    B, HP, L = qt.shape
    return pl.pallas_call(
        functools.partial(_attn_kernel, tk=tk, nb=nb),
        grid=(B, MLA_HEADS, L // tq),
        in_specs=[pl.BlockSpec((1, HEAD_PAD, tq), lambda b, h, i: (b, h, i)),
                  pl.BlockSpec((1, L, HEAD_PAD), lambda b, h, i: (b, 0, h)),
                  pl.BlockSpec((1, V_ROWS, L), lambda b, h, i: (b, h, 0))],
        out_specs=pl.BlockSpec((1, MLA_V, tq), lambda b, h, i: (b, h, i)),
        out_shape=jax.ShapeDtypeStruct((B, MLA_HEADS * MLA_V, L), F32),
        compiler_params=_params(("parallel", "parallel", "parallel")),
        name="attn",
    )(qt, k, vt)


def _gdn_prep_kernel(x_ref, xp_ref, xn_ref, ab_ref, cw_ref, alog_ref, dtb_ref,
                     q_ref, k_ref, v_ref, g_ref):
    i = pl.program_id(1)
    n = pl.num_programs(1)
    x = x_ref[0]
    tm = x.shape[0]
    prev = jnp.where(i == 0, 0.0, xp_ref[0])
    nxt = jnp.where(i == n - 1, 0.0, xn_ref[0])
    ext = jnp.concatenate([prev, x, nxt], axis=0)
    half = CONV_K // 2
    acc = None
    for j in range(CONV_K):
        sh = (half - j) % (tm + 16)
        xs = ext if sh == 0 else pltpu.roll(ext, sh, 0)
        term = xs[8:8 + tm, :] * cw_ref[j:j + 1, :]
        acc = term if acc is None else acc + term
    y = _silu(acc)
    nqk = GDN_HEADS * GDN_D
    for h in range(GDN_HEADS):
        qh = y[:, h * GDN_D:(h + 1) * GDN_D]
        kh = y[:, nqk + h * GDN_D:nqk + (h + 1) * GDN_D]
        qn = qh * lax.rsqrt(jnp.sum(qh * qh, axis=-1, keepdims=True) + EPS) * (GDN_D ** -0.5)
        kn = kh * lax.rsqrt(jnp.sum(kh * kh, axis=-1, keepdims=True) + EPS)
        q_ref[0, :, h * GDN_D:(h + 1) * GDN_D] = qn
        k_ref[0, :, h * GDN_D:(h + 1) * GDN_D] = kn
    v_ref[0] = y[:, 2 * nqk:]
    ab = ab_ref[0]
    gate = -jnp.exp(alog_ref[...]) * jax.nn.softplus(ab + dtb_ref[...])
    beta = jax.nn.sigmoid(ab)
    lane = lax.broadcasted_iota(jnp.int32, ab.shape, 1)
    g_ref[0] = jnp.where(lane < 2 * GDN_HEADS, gate, beta)


def _gdn_prep(qkv, ab, conv_w, alog, dtb, tm):
    B, L, C = qkv.shape
    n8 = L // 8
    t8 = tm // 8
    nd = GDN_HEADS * GDN_D
    tok = lambda n: pl.BlockSpec((1, tm, n), lambda b, i: (b, i, 0))
    return pl.pallas_call(
        _gdn_prep_kernel,
        grid=(B, L // tm),
        in_specs=[tok(C),
                  pl.BlockSpec((1, 8, C), lambda b, i: (b, jnp.maximum(i * t8 - 1, 0), 0)),
                  pl.BlockSpec((1, 8, C), lambda b, i: (b, jnp.minimum((i + 1) * t8, n8 - 1), 0)),
                  tok(LANE), _const_spec((8, C)), _const_spec((1, LANE)), _const_spec((1, LANE))],
        out_specs=[tok(nd), tok(nd), tok(nd), tok(LANE)],
        out_shape=[jax.ShapeDtypeStruct((B, L, nd), F32)] * 3 + [jax.ShapeDtypeStruct((B, L, LANE), F32)],
        compiler_params=_params(("parallel", "parallel")),
        name="gdn_prep",
    )(qkv, qkv, qkv, ab, conv_w, alog, dtb)


def _split(x):
    hi = x.astype(BF16)
    lo = (x - hi.astype(F32)).astype(BF16)
    return hi, lo


def _bmm(a, b):
    return lax.dot_general(a, b, (((2,), (1,)), ((0,), (0,))), preferred_element_type=F32)


def _bmm_nt(a, b):
    return lax.dot_general(a, b, (((2,), (2,)), ((0,), (0,))), preferred_element_type=F32)


def _tri_inv(a, eye, level_masks):
    d = eye - jnp.where(level_masks[0], a, 0.0)
    for m in level_masks[1:]:
        e = jnp.where(m, a, 0.0).astype(BF16)
        db = d.astype(BF16)
        d = d - _bmm(db, _bmm(e, db).astype(BF16))
    mh, ml = _split(eye + a)
    dh, dl = _split(d)
    r = eye - (_bmm(mh, dh) + (_bmm(mh, dl) + _bmm(ml, dh)))
    return d + _bmm(dh, r.astype(BF16))


def _gdn_kernel(qf_ref, kf_ref, vf_ref, gf_ref, qb_ref, kb_ref, vb_ref, gb_ref, of_ref, ob_ref, s_ref, *, cg):
    @pl.when(pl.program_id(1) == 0)
    def _():
        s_ref[...] = jnp.zeros_like(s_ref)

    C = CHUNK
    NH = GDN_HEADS
    refs = ((qf_ref, kf_ref, vf_ref, gf_ref, of_ref), (qb_ref, kb_ref, vb_ref, gb_ref, ob_ref))
    row = lax.broadcasted_iota(jnp.int32, (C, C), 0)
    col = lax.broadcasted_iota(jnp.int32, (C, C), 1)
    eye = (row == col).astype(F32)
    level_masks = []
    s = 1
    while s < C:
        level_masks.append(((row // (2 * s)) == (col // (2 * s))) & ((row // s) != (col // s)))
        s *= 2

    def dir_select(n, per_dir, when_bwd, when_fwd):
        idx = lax.broadcasted_iota(jnp.int32, (n, C, C), 0)
        is_bwd = (idx // per_dir) % 2 == 1
        return (is_bwd & when_bwd[None]) | (~is_bwd & when_fwd[None])

    def chunk_rows(j, d):
        c = cg - 1 - j if d else j
        return slice(c * C, (c + 1) * C)

    pairs = [(j, d) for j in range(cg) for d in range(2)]
    units = [(j, d, h) for (j, d) in pairs for h in range(NH)]
    g_all = jnp.stack([refs[d][3][0, chunk_rows(j, d), :] for (j, d) in pairs])
    hi = g_all.astype(BF16)
    r1 = g_all - hi.astype(F32)
    mid = r1.astype(BF16)
    lo = (r1 - mid.astype(F32)).astype(BF16)
    tri = dir_select(len(pairs), 1, row <= col, row >= col).astype(BF16)
    gc_all = _bmm(tri, hi) + (_bmm(tri, mid) + _bmm(tri, lo))
    gc_t = [gc_all[p].T for p in range(len(pairs))]

    def per_unit(fn):
        return jnp.stack([fn(j, d, h, j * 2 + d, d * NH + h) for (j, d, h) in units])

    hs = lambda h: slice(h * GDN_D, (h + 1) * GDN_D)
    q = per_unit(lambda j, d, h, p, cgate: refs[d][0][0, chunk_rows(j, d), hs(h)])
    k = per_unit(lambda j, d, h, p, cgate: refs[d][1][0, chunk_rows(j, d), hs(h)])
    v = per_unit(lambda j, d, h, p, cgate: refs[d][2][0, chunk_rows(j, d), hs(h)])
    gc = per_unit(lambda j, d, h, p, cgate: gc_all[p][:, cgate:cgate + 1])
    gr = per_unit(lambda j, d, h, p, cgate: gc_t[p][cgate:cgate + 1, :])
    beta = per_unit(lambda j, d, h, p, cgate: g_all[p][:, 2 * NH + cgate:2 * NH + cgate + 1])
    glast = per_unit(lambda j, d, h, p, cgate: gc_all[p][(0 if d else C - 1):(1 if d else C), cgate:cgate + 1])

    U = len(units)
    incl = dir_select(U, NH, row <= col, row >= col)
    strict = dir_select(U, NH, row < col, row > col)
    decay = jnp.where(incl, jnp.exp(jnp.where(incl, gc - gr, 0.0)), 0.0)
    k16 = k.astype(BF16)
    a = jnp.where(strict, _bmm_nt(k16, k16) * decay * beta, 0.0)
    t = _tri_inv(a, eye, level_masks).astype(BF16)
    eg = jnp.exp(gc)
    kv = jnp.concatenate([k * (beta * eg), v * beta], axis=2).astype(BF16)
    wu = _bmm(t, kv)
    qk = jnp.where(incl, _bmm_nt(q.astype(BF16), k16) * decay, 0.0).astype(BF16)
    wq = jnp.concatenate([wu[:, :, :GDN_D], q * eg], axis=1).astype(BF16)
    u_part = wu[:, :, GDN_D:]
    kg = k * jnp.exp(glast - gc)
    kg_t = jnp.stack([kg[i].T for i in range(U)]).astype(BF16)
    dlast = jnp.exp(glast)

    st = s_ref[...]
    n_chain = 2 * NH
    for j in range(cg):
        sl = slice(j * n_chain, (j + 1) * n_chain)
        ws_qs = _bmm(wq[sl], st.astype(BF16))
        v_new = u_part[sl] - ws_qs[:, :C]
        vn16 = v_new.astype(BF16)
        o = ws_qs[:, C:] + _bmm(qk[sl], vn16)
        st = st * dlast[sl] + _bmm(kg_t[sl], vn16)
        for d in range(2):
            for h in range(NH):
                refs[d][4][0, chunk_rows(j, d), hs(h)] = o[d * NH + h]
    s_ref[...] = st


def _gdn_scan(q, k, v, g, cg):
    B, L, nd = q.shape
    tm = cg * CHUNK
    n = L // tm
    fwd = lambda w: pl.BlockSpec((1, tm, w), lambda b, i: (b, i, 0))
    bwd = lambda w: pl.BlockSpec((1, tm, w), lambda b, i: (b, n - 1 - i, 0))
    out = jax.ShapeDtypeStruct((B, L, nd), F32)
    return pl.pallas_call(
        functools.partial(_gdn_kernel, cg=cg),
        grid=(B, n),
        in_specs=[fwd(nd), fwd(nd), fwd(nd), fwd(LANE), bwd(nd), bwd(nd), bwd(nd), bwd(LANE)],
        out_specs=[fwd(nd), bwd(nd)],
        out_shape=[out, out],
        scratch_shapes=[pltpu.VMEM((2 * GDN_HEADS, GDN_D, GDN_D), F32)],
        compiler_params=_params(("parallel", "arbitrary")),
        name="gdn_scan",
    )(q, k, v, g, q, k, v, g)


def _mix_kernel(h_ref, ot_ref, of_ref, ob_ref, z_ref, ga_ref, gg_ref, wa_ref, wb_ref, gp_ref, out_ref):
    ya = _rms(ot_ref[0].T, ga_ref[...]).astype(BF16)
    o = of_ref[0] + ob_ref[0]
    z = z_ref[0]
    gg = gg_ref[...]
    yb = jnp.concatenate(
        [_rms(o[:, h * GDN_D:(h + 1) * GDN_D], gg) * _silu(z[:, h * GDN_D:(h + 1) * GDN_D])
         for h in range(GDN_HEADS)], axis=1).astype(BF16)
    mix = _dot(ya, wa_ref[...]) + _dot(yb, wb_ref[...])
    out_ref[0] = h_ref[0] + _rms(mix, gp_ref[...])


def _mix(h, ot, of, ob, z, w, tm):
    B, L, D = h.shape
    na = ot.shape[1]
    nb = of.shape[2]
    tok = lambda n: pl.BlockSpec((1, tm, n), lambda b, i: (b, i, 0))
    return pl.pallas_call(
        _mix_kernel,
        grid=(B, L // tm),
        in_specs=[tok(D), pl.BlockSpec((1, na, tm), lambda b, i: (b, 0, i)), tok(nb), tok(nb), tok(nb),
                  _const_spec((1, na)), _const_spec((1, GDN_D)), _const_spec((na, D)), _const_spec((nb, D)),
                  _const_spec((1, D))],
        out_specs=tok(D),
        out_shape=jax.ShapeDtypeStruct((B, L, D), F32),
        compiler_params=_params(("parallel", "parallel")),
        name="mix",
    )(h, ot, of, ob, z, w["mla_out_g"], w["gdn_out_g"], w["wo_a"], w["wo_b"], w["mix_post_g"])


def _pad_lanes(a, left, width):
    return jnp.pad(a, ((0, 0), (left, width - left - a.shape[1])))


def _rot_half_cols(wmat):
    half = wmat.shape[-1] // 2
    return jnp.concatenate([-wmat[..., half:], wmat[..., :half]], axis=-1)


def _layer_weights(l, L, ffn1_pre_g, ffn1_w_gate, ffn1_w_up, ffn1_w_down, ffn1_post_g, mix_pre_g, w_in,
                   mla_q_norm_g, mla_w_uq, mla_kv_norm_g, mla_w_ukv, mla_out_norm_g, gdn_conv_w, gdn_a_log,
                   gdn_dt_bias, gdn_out_norm_g, w_out, mix_post_g, ffn2_pre_g, ffn2_w_gate, ffn2_w_up,
                   ffn2_w_down, ffn2_post_g, final_norm_g):
    row = lambda g: g[l][None, :].astype(F32)
    q_lora = mla_w_uq.shape[1]
    kv_lora = mla_w_ukv.shape[1]
    conv_ch = gdn_conv_w.shape[2]
    nz = GDN_HEADS * GDN_D
    c0 = q_lora
    c1 = c0 + kv_lora
    c2 = c1 + MLA_ROPE
    c3 = c2 + conv_ch
    c4 = c3 + nz
    wi = w_in[l]
    w_kr = wi[:, c1:c2]
    w_in_pad = jnp.concatenate([
        wi[:, :c1],
        _pad_lanes(w_kr, MLA_NOPE, LANE),
        _pad_lanes(_rot_half_cols(w_kr), MLA_NOPE, LANE),
        wi[:, c2:c4],
        _pad_lanes(wi[:, c4:], 0, LANE)], axis=1).astype(BF16)
    s_krr = c1 + 2 * LANE
    splits = (c0, c1, c1 + LANE, s_krr, s_krr + conv_ch, s_krr + conv_ch + nz)

    uq = mla_w_uq[l].reshape(q_lora, MLA_HEADS, MLA_QK)
    wq_a = jnp.pad(uq, ((0, 0), (0, 0), (0, HEAD_PAD - MLA_QK))).reshape(q_lora, -1).astype(BF16)
    wq_b = jnp.pad(_rot_half_cols(uq[..., MLA_NOPE:]),
                   ((0, 0), (0, 0), (MLA_NOPE, HEAD_PAD - MLA_QK))).reshape(q_lora, -1).astype(BF16)
    ukv = mla_w_ukv[l].reshape(kv_lora, MLA_HEADS, MLA_NOPE + MLA_V)
    wk = jnp.pad(ukv[..., :MLA_NOPE], ((0, 0), (0, 0), (0, HEAD_PAD - MLA_NOPE))).reshape(kv_lora, -1).astype(BF16)
    wv = ukv[..., MLA_NOPE:].reshape(kv_lora, -1).astype(BF16)

    inv = ROPE_THETA ** (-jnp.arange(0, MLA_ROPE, 2, dtype=F32) / MLA_ROPE)
    ang = jnp.arange(L, dtype=F32)[:, None] * inv[None, :]
    cos, sin = jnp.cos(ang), jnp.sin(ang)
    cos_k = jnp.concatenate([jnp.ones((L, MLA_NOPE), F32), cos, cos,
                             jnp.zeros((L, HEAD_PAD - MLA_QK), F32)], axis=1)
    sin_k = _pad_lanes(jnp.concatenate([sin, sin], axis=1), MLA_NOPE, LANE)
    qscale = (MLA_QK ** -0.5) * math.log2(math.e)

    alog = _pad_lanes(gdn_a_log[l].reshape(1, -1).astype(F32), 0, LANE)
    dtb = _pad_lanes(gdn_dt_bias[l].reshape(1, -1).astype(F32), 0, LANE)
    n_a = mla_out_norm_g.shape[1]
    return dict(
        ffn1=(row(ffn1_pre_g), ffn1_w_gate[l].astype(BF16), ffn1_w_up[l].astype(BF16),
              ffn1_w_down[l].astype(BF16), row(ffn1_post_g)),
        ffn2=(row(ffn2_pre_g), ffn2_w_gate[l].astype(BF16), ffn2_w_up[l].astype(BF16),
              ffn2_w_down[l].astype(BF16), row(ffn2_post_g)),
        final_g=row(final_norm_g),
        mix_pre_g=row(mix_pre_g), w_in=w_in_pad, splits=splits,
        q_norm_g=row(mla_q_norm_g), wq_a=wq_a, wq_b=wq_b, kv_norm_g=row(mla_kv_norm_g), wk=wk, wv=wv,
        cos_q=cos_k * qscale, sin_q=sin_k * qscale, cos_k=cos_k, sin_k=sin_k,
        conv_w=jnp.pad(gdn_conv_w[l].astype(F32), ((0, 8 - CONV_K), (0, 0))), alog=alog, dtb=dtb,
        mla_out_g=row(mla_out_norm_g), gdn_out_g=row(gdn_out_norm_g),
        wo_a=w_out[l][:n_a].astype(BF16), wo_b=w_out[l][n_a:].astype(BF16), mix_post_g=row(mix_post_g),
    )


def _tile(L, pref):
    return pref if L % pref == 0 else L


def _layer(h, w, last):
    B, L, D = h.shape
    tm = _tile(L, 512)
    h = _ffn(h.reshape(B * L, D), *w["ffn1"], w["final_g"], False, tm).reshape(B, L, D)
    qt, k, vt, qkv, z, ab = _proj(h, w, tm)
    tk = _tile(L, ATTN_TK)
    nb = ATTN_NB if L % (tk * ATTN_NB) == 0 else 1
    ot = _attn(qt, k, vt, _tile(L, ATTN_TQ), tk, nb)
    gq, gk, gv, gg = _gdn_prep(qkv, ab, w["conv_w"], w["alog"], w["dtb"], tm)
    of, ob = _gdn_scan(gq, gk, gv, gg, GDN_CG)
    h = _mix(h, ot, of, ob, z, w, tm)
    h = _ffn(h.reshape(B * L, D), *w["ffn2"], w["final_g"], last, tm).reshape(B, L, D)
    return h


def _trunk(x, weights):
    depth = weights[0].shape[0]
    h = x
    for l in range(depth):
        h = _layer(h, _layer_weights(l, x.shape[1], *weights), l == depth - 1)
    return h


def kernel(x_prompt, x_sample, ffn1_pre_g, ffn1_w_gate, ffn1_w_up, ffn1_w_down, ffn1_post_g, mix_pre_g, w_in, mla_q_norm_g, mla_w_uq, mla_kv_norm_g, mla_w_ukv, mla_out_norm_g, gdn_conv_w, gdn_a_log, gdn_dt_bias, gdn_out_norm_g, w_out, mix_post_g, ffn2_pre_g, ffn2_w_gate, ffn2_w_up, ffn2_w_down, ffn2_post_g, final_norm_g):
    weights = (ffn1_pre_g, ffn1_w_gate, ffn1_w_up, ffn1_w_down, ffn1_post_g, mix_pre_g, w_in,
               mla_q_norm_g, mla_w_uq, mla_kv_norm_g, mla_w_ukv, mla_out_norm_g, gdn_conv_w, gdn_a_log,
               gdn_dt_bias, gdn_out_norm_g, w_out, mix_post_g, ffn2_pre_g, ffn2_w_gate, ffn2_w_up,
               ffn2_w_down, ffn2_post_g, final_norm_g)
    return (_trunk(x_prompt, weights), _trunk(x_sample, weights))
```

```python
import functools
import math

import jax
import jax.numpy as jnp
from jax import lax
from jax.experimental import pallas as pl
from jax.experimental.pallas import tpu as pltpu

F32 = jnp.float32
BF16 = jnp.bfloat16

EPS = 1e-6
MLA_HEADS = 8
MLA_NOPE = 64
MLA_ROPE = 32
MLA_QK = MLA_NOPE + MLA_ROPE
MLA_V = 64
HEAD_PAD = 128
V_ROWS = 80
ROPE_THETA = 10000.0
GDN_HEADS = 4
GDN_D = 128
CONV_K = 5
CHUNK = 64
LANE = 128

ATTN_TQ = 1024
ATTN_TK = 256
ATTN_NB = 16
ATTN_AHEAD = 2
ATTN_SAFE_SHIFT = 40.0
ATTN_BOUND_SLACK = 1.01
GDN_CG = 4

VMEM_LIMIT = 56 * 1024 * 1024


def _dot(a, b):
    return jnp.dot(a, b, preferred_element_type=F32)


def _dot_nt(a, b):
    return lax.dot_general(a, b, (((1,), (1,)), ((), ())), preferred_element_type=F32)


def _rms(x, g):
    ms = jnp.mean(x * x, axis=-1, keepdims=True)
    return x * lax.rsqrt(ms + EPS) * g


def _silu(x):
    return x * jax.nn.sigmoid(x)


def _const_spec(shape):
    nd = len(shape)
    return pl.BlockSpec(shape, lambda *_: (0,) * nd, pipeline_mode=pl.Buffered(1))


def _params(sem):
    return pltpu.CompilerParams(dimension_semantics=sem, vmem_limit_bytes=VMEM_LIMIT)


def _ffn_kernel(x_ref, pre_ref, wg_ref, wu_ref, wd_ref, post_ref, fin_ref, o_ref, *, final):
    x = x_ref[...]
    xn = _rms(x, pre_ref[...]).astype(BF16)
    g = _dot(xn, wg_ref[...])
    u = _dot(xn, wu_ref[...])
    a = (_silu(g) * u).astype(BF16)
    d = _dot(a, wd_ref[...])
    h = x + 0.5 * _rms(d, post_ref[...])
    if final:
        h = _rms(h, fin_ref[...])
    o_ref[...] = h


def _ffn(x2d, pre_g, wg, wu, wd, post_g, fin_g, final, tm):
    T, D = x2d.shape
    F = wg.shape[1]
    row = pl.BlockSpec((tm, D), lambda i: (i, 0))
    return pl.pallas_call(
        functools.partial(_ffn_kernel, final=final),
        grid=(T // tm,),
        in_specs=[row, _const_spec((1, D)), _const_spec((D, F)), _const_spec((D, F)),
                  _const_spec((F, D)), _const_spec((1, D)), _const_spec((1, D))],
        out_specs=row,
        out_shape=jax.ShapeDtypeStruct((T, D), F32),
        compiler_params=_params(("parallel",)),
        name="ffn_final" if final else "ffn",
    )(x2d, pre_g, wg, wu, wd, post_g, fin_g)


def _proj_kernel(h_ref, pre_ref, win_ref, gq_ref, wqa_ref, wqb_ref, gkv_ref, wk_ref, wv_ref,
                 hsel_ref, cq_ref, sq_ref, ck_ref, sk_ref,
                 qt_ref, k_ref, vt_ref, kn_ref, qkv_ref, z_ref, ab_ref, *, splits):
    s_cq, s_ckv, s_kr, s_krr, s_qkv, s_z = splits
    u = _rms(h_ref[0], pre_ref[...]).astype(BF16)
    proj = _dot(u, win_ref[...])
    qkv_ref[0] = proj[:, s_krr:s_qkv]
    z_ref[0] = proj[:, s_qkv:s_z]
    ab_ref[0] = proj[:, s_z:]

    cqn = _rms(proj[:, :s_cq], gq_ref[...]).astype(BF16)
    qa = _dot(cqn, wqa_ref[...])
    qb = _dot(cqn, wqb_ref[...])
    cq, sq = cq_ref[...], sq_ref[...]
    q = jnp.concatenate(
        [qa[:, h * LANE:(h + 1) * LANE] * cq + qb[:, h * LANE:(h + 1) * LANE] * sq
         for h in range(MLA_HEADS)], axis=1)
    qt_ref[0] = q.T.astype(BF16)

    ckvn = _rms(proj[:, s_cq:s_ckv], gkv_ref[...]).astype(BF16)
    kpe = proj[:, s_ckv:s_kr] * ck_ref[...] + proj[:, s_kr:s_krr] * sk_ref[...]
    kb = _dot(ckvn, wk_ref[...])
    kf = jnp.concatenate([kb[:, h * LANE:(h + 1) * LANE] + kpe for h in range(MLA_HEADS)], axis=1)
    k_ref[0] = kf.astype(BF16)
    n2 = _dot((kf * kf).astype(BF16), hsel_ref[...])
    kn_ref[0] = jnp.broadcast_to(jnp.max(n2, axis=0, keepdims=True), (8, LANE))

    vt = _dot(ckvn, wv_ref[...]).T.astype(BF16)
    tm = vt.shape[1]
    ones = jnp.ones((V_ROWS - MLA_V, tm), BF16)
    for h in range(MLA_HEADS):
        vt_ref[0, h * V_ROWS:h * V_ROWS + MLA_V, :] = vt[h * MLA_V:(h + 1) * MLA_V, :]
        vt_ref[0, h * V_ROWS + MLA_V:(h + 1) * V_ROWS, :] = ones


def _proj(h, w, tm):
    B, L, D = h.shape
    ncol = w["w_in"].shape[1]
    HP = MLA_HEADS * HEAD_PAD
    splits = w["splits"]
    n_qkv = splits[4] - splits[3]
    n_z = splits[5] - splits[4]
    tok = lambda n: pl.BlockSpec((1, tm, n), lambda b, i: (b, i, 0))
    tokT = lambda n: pl.BlockSpec((1, n, tm), lambda b, i: (b, 0, i))
    tab = pl.BlockSpec((tm, LANE), lambda b, i: (i, 0))
    cq_lora = w["wq_a"].shape[0]
    ckv_lora = w["wk"].shape[0]
    return pl.pallas_call(
        functools.partial(_proj_kernel, splits=splits),
        grid=(B, L // tm),
        in_specs=[tok(D), _const_spec((1, D)), _const_spec((D, ncol)),
                  _const_spec((1, cq_lora)), _const_spec((cq_lora, HP)), _const_spec((cq_lora, HP)),
                  _const_spec((1, ckv_lora)), _const_spec((ckv_lora, HP)),
                  _const_spec((ckv_lora, MLA_HEADS * MLA_V)), _const_spec((HP, LANE)),
                  tab, tab, tab, tab],
        out_specs=[tokT(HP), tok(HP), tokT(MLA_HEADS * V_ROWS),
                   pl.BlockSpec((1, 8, LANE), lambda b, i: (b, i, 0)),
                   tok(n_qkv), tok(n_z), tok(LANE)],
        out_shape=[jax.ShapeDtypeStruct((B, HP, L), BF16),
                   jax.ShapeDtypeStruct((B, L, HP), BF16),
                   jax.ShapeDtypeStruct((B, MLA_HEADS * V_ROWS, L), BF16),
                   jax.ShapeDtypeStruct((B, 8 * (L // tm), LANE), F32),
                   jax.ShapeDtypeStruct((B, L, n_qkv), F32),
                   jax.ShapeDtypeStruct((B, L, n_z), F32),
                   jax.ShapeDtypeStruct((B, L, LANE), F32)],
        compiler_params=_params(("parallel", "parallel")),
        name="proj",
    )(h, w["mix_pre_g"], w["w_in"], w["q_norm_g"], w["wq_a"], w["wq_b"], w["kv_norm_g"], w["wk"], w["wv"],
      w["head_sel"], w["cos_q"], w["sin_q"], w["cos_k"], w["sin_k"])


def _attn_kernel(qt_ref, k_ref, vt_ref, kn_ref, o_ref, *, tk, nb):
    qt = qt_ref[0]
    tq = qt.shape[1]
    step = tk * nb
    nk = k_ref.shape[1] // step
    acc0 = jnp.zeros((V_ROWS, tq), F32)

    def scores(off, i):
        return _dot(k_ref[0, pl.ds(off + i * tk, tk), :], qt)

    qf = qt.astype(F32)
    kn = kn_ref[0]
    head_lane = lax.broadcasted_iota(jnp.int32, kn.shape, 1) == pl.program_id(1)
    k2max = jnp.max(jnp.where(head_lane, kn, 0.0))
    bound = jnp.sqrt(jnp.sum(qf * qf, axis=0, keepdims=True) * k2max) * ATTN_BOUND_SLACK

    def fixed_shift():
        def body(j, acc):
            off = pl.multiple_of(j * step, step)
            ss = [scores(off, i) for i in range(min(ATTN_AHEAD, nb))]
            for i in range(nb):
                if i + ATTN_AHEAD < nb:
                    ss.append(scores(off, i + ATTN_AHEAD))
                p = jnp.exp2(ss[i] - bound).astype(BF16)
                acc = acc + _dot(vt_ref[0, :, pl.ds(off + i * tk, tk)], p)
            return acc
        return lax.fori_loop(0, nk, body, acc0)

    def running_max():
        def body(j, carry):
            m, acc = carry
            off = pl.multiple_of(j * step, step)
            ss = [scores(off, i) for i in range(min(ATTN_AHEAD, nb))]
            for i in range(nb):
                if i + ATTN_AHEAD < nb:
                    ss.append(scores(off, i + ATTN_AHEAD))
                m_new = jnp.maximum(m, jnp.max(ss[i], axis=0, keepdims=True))
                p = jnp.exp2(ss[i] - m_new).astype(BF16)
                acc = acc * jnp.exp2(m - m_new) + _dot(vt_ref[0, :, pl.ds(off + i * tk, tk)], p)
                m = m_new
            return m, acc
        return lax.fori_loop(0, nk, body, (jnp.full((1, tq), -1e30, F32), acc0))[1]

    acc = lax.cond(jnp.max(bound) <= ATTN_SAFE_SHIFT, fixed_shift, running_max)
    o_ref[0] = acc[:MLA_V, :] / acc[MLA_V:MLA_V + 1, :]


def _attn(qt, k, vt, kn, tq, tk, nb):
    B, HP, L = qt.shape
    return pl.pallas_call(
        functools.partial(_attn_kernel, tk=tk, nb=nb),
        grid=(B, MLA_HEADS, L // tq),
        in_specs=[pl.BlockSpec((1, HEAD_PAD, tq), lambda b, h, i: (b, h, i)),
                  pl.BlockSpec((1, L, HEAD_PAD), lambda b, h, i: (b, 0, h)),
                  pl.BlockSpec((1, V_ROWS, L), lambda b, h, i: (b, h, 0)),
                  pl.BlockSpec((1, kn.shape[1], LANE), lambda b, h, i: (b, 0, 0))],
        out_specs=pl.BlockSpec((1, MLA_V, tq), lambda b, h, i: (b, h, i)),
        out_shape=jax.ShapeDtypeStruct((B, MLA_HEADS * MLA_V, L), F32),
        compiler_params=_params(("parallel", "parallel", "parallel")),
        name="attn",
    )(qt, k, vt, kn)


def _gdn_prep_kernel(x_ref, xp_ref, xn_ref, ab_ref, cw_ref, alog_ref, dtb_ref,
                     q_ref, k_ref, v_ref, g_ref):
    i = pl.program_id(1)
    n = pl.num_programs(1)
    x = x_ref[0]
    tm = x.shape[0]
    prev = jnp.where(i == 0, 0.0, xp_ref[0])
    nxt = jnp.where(i == n - 1, 0.0, xn_ref[0])
    ext = jnp.concatenate([prev, x, nxt], axis=0)
    half = CONV_K // 2
    acc = None
    for j in range(CONV_K):
        sh = (half - j) % (tm + 16)
        xs = ext if sh == 0 else pltpu.roll(ext, sh, 0)
        term = xs[8:8 + tm, :] * cw_ref[j:j + 1, :]
        acc = term if acc is None else acc + term
    y = _silu(acc)
    nqk = GDN_HEADS * GDN_D
    for h in range(GDN_HEADS):
        qh = y[:, h * GDN_D:(h + 1) * GDN_D]
        kh = y[:, nqk + h * GDN_D:nqk + (h + 1) * GDN_D]
        qn = qh * lax.rsqrt(jnp.sum(qh * qh, axis=-1, keepdims=True) + EPS) * (GDN_D ** -0.5)
        kn = kh * lax.rsqrt(jnp.sum(kh * kh, axis=-1, keepdims=True) + EPS)
        q_ref[0, :, h * GDN_D:(h + 1) * GDN_D] = qn
        k_ref[0, :, h * GDN_D:(h + 1) * GDN_D] = kn
    v_ref[0] = y[:, 2 * nqk:]
    ab = ab_ref[0]
    gate = -jnp.exp(alog_ref[...]) * jax.nn.softplus(ab + dtb_ref[...])
    beta = jax.nn.sigmoid(ab)
    lane = lax.broadcasted_iota(jnp.int32, ab.shape, 1)
    g_ref[0] = jnp.where(lane < 2 * GDN_HEADS, gate, beta)


def _gdn_prep(qkv, ab, conv_w, alog, dtb, tm):
    B, L, C = qkv.shape
    n8 = L // 8
    t8 = tm // 8
    nd = GDN_HEADS * GDN_D
    tok = lambda n: pl.BlockSpec((1, tm, n), lambda b, i: (b, i, 0))
    return pl.pallas_call(
        _gdn_prep_kernel,
        grid=(B, L // tm),
        in_specs=[tok(C),
                  pl.BlockSpec((1, 8, C), lambda b, i: (b, jnp.maximum(i * t8 - 1, 0), 0)),
                  pl.BlockSpec((1, 8, C), lambda b, i: (b, jnp.minimum((i + 1) * t8, n8 - 1), 0)),
                  tok(LANE), _const_spec((8, C)), _const_spec((1, LANE)), _const_spec((1, LANE))],
        out_specs=[tok(nd), tok(nd), tok(nd), tok(LANE)],
        out_shape=[jax.ShapeDtypeStruct((B, L, nd), F32)] * 3 + [jax.ShapeDtypeStruct((B, L, LANE), F32)],
        compiler_params=_params(("parallel", "parallel")),
        name="gdn_prep",
    )(qkv, qkv, qkv, ab, conv_w, alog, dtb)


def _split(x):
    hi = x.astype(BF16)
    lo = (x - hi.astype(F32)).astype(BF16)
    return hi, lo


def _bmm(a, b):
    return lax.dot_general(a, b, (((2,), (1,)), ((0,), (0,))), preferred_element_type=F32)


def _bmm_nt(a, b):
    return lax.dot_general(a, b, (((2,), (2,)), ((0,), (0,))), preferred_element_type=F32)


def _tri_inv(a, eye, level_masks):
    d = eye - jnp.where(level_masks[0], a, 0.0)
    a16 = a.astype(BF16)
    for m in level_masks[1:]:
        db = d.astype(BF16)
        d = d - jnp.where(m, _bmm(db, _bmm(a16, db).astype(BF16)), 0.0)
    mh, ml = _split(eye + a)
    x = d.astype(BF16)
    r = eye - (_bmm(mh, x) + _bmm(ml, x))
    return x.astype(F32) + _bmm(x, r.astype(BF16))


def _gdn_kernel(qf_ref, kf_ref, vf_ref, gf_ref, qb_ref, kb_ref, vb_ref, gb_ref, of_ref, ob_ref, s_ref, *, cg):
    @pl.when(pl.program_id(1) == 0)
    def _():
        s_ref[...] = jnp.zeros_like(s_ref)

    C = CHUNK
    NH = GDN_HEADS
    refs = ((qf_ref, kf_ref, vf_ref, gf_ref, of_ref), (qb_ref, kb_ref, vb_ref, gb_ref, ob_ref))
    row = lax.broadcasted_iota(jnp.int32, (C, C), 0)
    col = lax.broadcasted_iota(jnp.int32, (C, C), 1)
    eye = (row == col).astype(F32)
    level_masks = []
    s = 1
    while s < C:
        level_masks.append(((row // (2 * s)) == (col // (2 * s))) & ((row // s) != (col // s)))
        s *= 2

    def dir_select(n, per_dir, when_bwd, when_fwd):
        idx = lax.broadcasted_iota(jnp.int32, (n, C, C), 0)
        is_bwd = (idx // per_dir) % 2 == 1
        return (is_bwd & when_bwd[None]) | (~is_bwd & when_fwd[None])

    def chunk_rows(j, d):
        c = cg - 1 - j if d else j
        return slice(c * C, (c + 1) * C)

    pairs = [(j, d) for j in range(cg) for d in range(2)]
    units = [(j, d, h) for (j, d) in pairs for h in range(NH)]
    g_all = jnp.stack([refs[d][3][0, chunk_rows(j, d), :] for (j, d) in pairs])
    hi = g_all.astype(BF16)
    r1 = g_all - hi.astype(F32)
    mid = r1.astype(BF16)
    lo = (r1 - mid.astype(F32)).astype(BF16)
    tri = dir_select(len(pairs), 1, row <= col, row >= col).astype(BF16)
    gc_all = _bmm(tri, hi) + (_bmm(tri, mid) + _bmm(tri, lo))
    glast_all = jnp.stack([gc_all[j * 2 + d][(0 if d else C - 1):(1 if d else C), :] for (j, d) in pairs])
    eg_all = jnp.exp(gc_all)
    kdec_all = jnp.exp(glast_all - gc_all)
    dlast_all = jnp.exp(glast_all)
    gc_t = [gc_all[p].T for p in range(len(pairs))]
    g_t = [g_all[p].T for p in range(len(pairs))]

    def per_unit(fn):
        return jnp.stack([fn(j, d, h, j * 2 + d, d * NH + h) for (j, d, h) in units])

    hs = lambda h: slice(h * GDN_D, (h + 1) * GDN_D)
    col1 = lambda arr, c: arr[:, c:c + 1]
    q = per_unit(lambda j, d, h, p, cgate: refs[d][0][0, chunk_rows(j, d), hs(h)])
    k = per_unit(lambda j, d, h, p, cgate: refs[d][1][0, chunk_rows(j, d), hs(h)])
    v = per_unit(lambda j, d, h, p, cgate: refs[d][2][0, chunk_rows(j, d), hs(h)])
    gc = per_unit(lambda j, d, h, p, cgate: col1(gc_all[p], cgate))
    gr = per_unit(lambda j, d, h, p, cgate: gc_t[p][cgate:cgate + 1, :])
    beta = per_unit(lambda j, d, h, p, cgate: col1(g_all[p], 2 * NH + cgate))
    beta_r = per_unit(lambda j, d, h, p, cgate: g_t[p][2 * NH + cgate:2 * NH + cgate + 1, :])
    eg = per_unit(lambda j, d, h, p, cgate: col1(eg_all[p], cgate))
    kdec = per_unit(lambda j, d, h, p, cgate: col1(kdec_all[p], cgate))
    dlast = per_unit(lambda j, d, h, p, cgate: col1(dlast_all[p], cgate))

    U = len(units)
    incl = dir_select(U, NH, row <= col, row >= col)
    strict = dir_select(U, NH, row < col, row > col)
    decay = jnp.where(incl, jnp.exp(jnp.where(incl, gc - gr, 0.0)), 0.0)
    k16 = k.astype(BF16)
    a = jnp.where(strict, _bmm_nt(k16, k16) * decay * beta, 0.0)
    tb = (_tri_inv(a, eye, level_masks) * beta_r).astype(BF16)
    kv = jnp.concatenate([k * eg, v], axis=2).astype(BF16)
    wu = _bmm(tb, kv)
    qk = jnp.where(incl, _bmm_nt(q.astype(BF16), k16) * decay, 0.0).astype(BF16)
    wq = jnp.concatenate([wu[:, :, :GDN_D], q * eg], axis=1).astype(BF16)
    u_part = wu[:, :, GDN_D:]
    kg = k * kdec
    kg_t = jnp.stack([kg[i].T for i in range(U)]).astype(BF16)

    st = s_ref[...]
    n_chain = 2 * NH
    for j in range(cg):
        sl = slice(j * n_chain, (j + 1) * n_chain)
        ws_qs = _bmm(wq[sl], st.astype(BF16))
        v_new = u_part[sl] - ws_qs[:, :C]
        vn16 = v_new.astype(BF16)
        o = ws_qs[:, C:] + _bmm(qk[sl], vn16)
        st = st * dlast[sl] + _bmm(kg_t[sl], vn16)
        for d in range(2):
            for h in range(NH):
                refs[d][4][0, chunk_rows(j, d), hs(h)] = o[d * NH + h]
    s_ref[...] = st


def _gdn_scan(q, k, v, g, cg):
    B, L, nd = q.shape
    tm = cg * CHUNK
    n = L // tm
    fwd = lambda w: pl.BlockSpec((1, tm, w), lambda b, i: (b, i, 0))
    bwd = lambda w: pl.BlockSpec((1, tm, w), lambda b, i: (b, n - 1 - i, 0))
    out = jax.ShapeDtypeStruct((B, L, nd), F32)
    return pl.pallas_call(
        functools.partial(_gdn_kernel, cg=cg),
        grid=(B, n),
        in_specs=[fwd(nd), fwd(nd), fwd(nd), fwd(LANE), bwd(nd), bwd(nd), bwd(nd), bwd(LANE)],
        out_specs=[fwd(nd), bwd(nd)],
        out_shape=[out, out],
        scratch_shapes=[pltpu.VMEM((2 * GDN_HEADS, GDN_D, GDN_D), F32)],
        compiler_params=_params(("parallel", "arbitrary")),
        name="gdn_scan",
    )(q, k, v, g, q, k, v, g)


def _mix_kernel(h_ref, ot_ref, of_ref, ob_ref, z_ref, ga_ref, gg_ref, wa_ref, wb_ref, gp_ref, out_ref):
    ya = _rms(ot_ref[0].T, ga_ref[...]).astype(BF16)
    o = of_ref[0] + ob_ref[0]
    z = z_ref[0]
    gg = gg_ref[...]
    yb = jnp.concatenate(
        [_rms(o[:, h * GDN_D:(h + 1) * GDN_D], gg) * _silu(z[:, h * GDN_D:(h + 1) * GDN_D])
         for h in range(GDN_HEADS)], axis=1).astype(BF16)
    mix = _dot(ya, wa_ref[...]) + _dot(yb, wb_ref[...])
    out_ref[0] = h_ref[0] + _rms(mix, gp_ref[...])


def _mix(h, ot, of, ob, z, w, tm):
    B, L, D = h.shape
    na = ot.shape[1]
    nb = of.shape[2]
    tok = lambda n: pl.BlockSpec((1, tm, n), lambda b, i: (b, i, 0))
    return pl.pallas_call(
        _mix_kernel,
        grid=(B, L // tm),
        in_specs=[tok(D), pl.BlockSpec((1, na, tm), lambda b, i: (b, 0, i)), tok(nb), tok(nb), tok(nb),
                  _const_spec((1, na)), _const_spec((1, GDN_D)), _const_spec((na, D)), _const_spec((nb, D)),
                  _const_spec((1, D))],
        out_specs=tok(D),
        out_shape=jax.ShapeDtypeStruct((B, L, D), F32),
        compiler_params=_params(("parallel", "parallel")),
        name="mix",
    )(h, ot, of, ob, z, w["mla_out_g"], w["gdn_out_g"], w["wo_a"], w["wo_b"], w["mix_post_g"])


def _pad_lanes(a, left, width):
    return jnp.pad(a, ((0, 0), (left, width - left - a.shape[1])))


def _rot_half_cols(wmat):
    half = wmat.shape[-1] // 2
    return jnp.concatenate([-wmat[..., half:], wmat[..., :half]], axis=-1)


def _layer_weights(l, L, ffn1_pre_g, ffn1_w_gate, ffn1_w_up, ffn1_w_down, ffn1_post_g, mix_pre_g, w_in,
                   mla_q_norm_g, mla_w_uq, mla_kv_norm_g, mla_w_ukv, mla_out_norm_g, gdn_conv_w, gdn_a_log,
                   gdn_dt_bias, gdn_out_norm_g, w_out, mix_post_g, ffn2_pre_g, ffn2_w_gate, ffn2_w_up,
                   ffn2_w_down, ffn2_post_g, final_norm_g):
    row = lambda g: g[l][None, :].astype(F32)
    q_lora = mla_w_uq.shape[1]
    kv_lora = mla_w_ukv.shape[1]
    conv_ch = gdn_conv_w.shape[2]
    nz = GDN_HEADS * GDN_D
    c0 = q_lora
    c1 = c0 + kv_lora
    c2 = c1 + MLA_ROPE
    c3 = c2 + conv_ch
    c4 = c3 + nz
    wi = w_in[l]
    w_kr = wi[:, c1:c2]
    w_in_pad = jnp.concatenate([
        wi[:, :c1],
        _pad_lanes(w_kr, MLA_NOPE, LANE),
        _pad_lanes(_rot_half_cols(w_kr), MLA_NOPE, LANE),
        wi[:, c2:c4],
        _pad_lanes(wi[:, c4:], 0, LANE)], axis=1).astype(BF16)
    s_krr = c1 + 2 * LANE
    splits = (c0, c1, c1 + LANE, s_krr, s_krr + conv_ch, s_krr + conv_ch + nz)

    uq = mla_w_uq[l].reshape(q_lora, MLA_HEADS, MLA_QK)
    wq_a = jnp.pad(uq, ((0, 0), (0, 0), (0, HEAD_PAD - MLA_QK))).reshape(q_lora, -1).astype(BF16)
    wq_b = jnp.pad(_rot_half_cols(uq[..., MLA_NOPE:]),
                   ((0, 0), (0, 0), (MLA_NOPE, HEAD_PAD - MLA_QK))).reshape(q_lora, -1).astype(BF16)
    ukv = mla_w_ukv[l].reshape(kv_lora, MLA_HEADS, MLA_NOPE + MLA_V)
    wk = jnp.pad(ukv[..., :MLA_NOPE], ((0, 0), (0, 0), (0, HEAD_PAD - MLA_NOPE))).reshape(kv_lora, -1).astype(BF16)
    wv = ukv[..., MLA_NOPE:].reshape(kv_lora, -1).astype(BF16)

    inv = ROPE_THETA ** (-jnp.arange(0, MLA_ROPE, 2, dtype=F32) / MLA_ROPE)
    ang = jnp.arange(L, dtype=F32)[:, None] * inv[None, :]
    cos, sin = jnp.cos(ang), jnp.sin(ang)
    cos_k = jnp.concatenate([jnp.ones((L, MLA_NOPE), F32), cos, cos,
                             jnp.zeros((L, HEAD_PAD - MLA_QK), F32)], axis=1)
    sin_k = _pad_lanes(jnp.concatenate([sin, sin], axis=1), MLA_NOPE, LANE)
    qscale = (MLA_QK ** -0.5) * math.log2(math.e)

    alog = _pad_lanes(gdn_a_log[l].reshape(1, -1).astype(F32), 0, LANE)
    dtb = _pad_lanes(gdn_dt_bias[l].reshape(1, -1).astype(F32), 0, LANE)
    n_a = mla_out_norm_g.shape[1]
    return dict(
        ffn1=(row(ffn1_pre_g), ffn1_w_gate[l].astype(BF16), ffn1_w_up[l].astype(BF16),
              ffn1_w_down[l].astype(BF16), row(ffn1_post_g)),
        ffn2=(row(ffn2_pre_g), ffn2_w_gate[l].astype(BF16), ffn2_w_up[l].astype(BF16),
              ffn2_w_down[l].astype(BF16), row(ffn2_post_g)),
        final_g=row(final_norm_g),
        mix_pre_g=row(mix_pre_g), w_in=w_in_pad, splits=splits,
        q_norm_g=row(mla_q_norm_g), wq_a=wq_a, wq_b=wq_b, kv_norm_g=row(mla_kv_norm_g), wk=wk, wv=wv,
        cos_q=cos_k * qscale, sin_q=sin_k * qscale, cos_k=cos_k, sin_k=sin_k,
        head_sel=(jnp.arange(MLA_HEADS * HEAD_PAD)[:, None] // HEAD_PAD == jnp.arange(LANE)[None, :]).astype(BF16),
        conv_w=jnp.pad(gdn_conv_w[l].astype(F32), ((0, 8 - CONV_K), (0, 0))), alog=alog, dtb=dtb,
        mla_out_g=row(mla_out_norm_g), gdn_out_g=row(gdn_out_norm_g),
        wo_a=w_out[l][:n_a].astype(BF16), wo_b=w_out[l][n_a:].astype(BF16), mix_post_g=row(mix_post_g),
    )


def _tile(L, pref):
    return pref if L % pref == 0 else L


def _layer(h, w, last):
    B, L, D = h.shape
    tm = _tile(L, 512)
    h = _ffn(h.reshape(B * L, D), *w["ffn1"], w["final_g"], False, tm).reshape(B, L, D)
    qt, k, vt, kn, qkv, z, ab = _proj(h, w, tm)
    tk = _tile(L, ATTN_TK)
    nb = ATTN_NB if L % (tk * ATTN_NB) == 0 else 1
    ot = _attn(qt, k, vt, kn, _tile(L, ATTN_TQ), tk, nb)
    gq, gk, gv, gg = _gdn_prep(qkv, ab, w["conv_w"], w["alog"], w["dtb"], tm)
    of, ob = _gdn_scan(gq, gk, gv, gg, GDN_CG)
    h = _mix(h, ot, of, ob, z, w, tm)
    h = _ffn(h.reshape(B * L, D), *w["ffn2"], w["final_g"], last, tm).reshape(B, L, D)
    return h


def _trunk(x, weights):
    depth = weights[0].shape[0]
    h = x
    for l in range(depth):
        h = _layer(h, _layer_weights(l, x.shape[1], *weights), l == depth - 1)
    return h


def kernel(x_prompt, x_sample, ffn1_pre_g, ffn1_w_gate, ffn1_w_up, ffn1_w_down, ffn1_post_g, mix_pre_g, w_in, mla_q_norm_g, mla_w_uq, mla_kv_norm_g, mla_w_ukv, mla_out_norm_g, gdn_conv_w, gdn_a_log, gdn_dt_bias, gdn_out_norm_g, w_out, mix_post_g, ffn2_pre_g, ffn2_w_gate, ffn2_w_up, ffn2_w_down, ffn2_post_g, final_norm_g):
    weights = (ffn1_pre_g, ffn1_w_gate, ffn1_w_up, ffn1_w_down, ffn1_post_g, mix_pre_g, w_in,
               mla_q_norm_g, mla_w_uq, mla_kv_norm_g, mla_w_ukv, mla_out_norm_g, gdn_conv_w, gdn_a_log,
               gdn_dt_bias, gdn_out_norm_g, w_out, mix_post_g, ffn2_pre_g, ffn2_w_gate, ffn2_w_up,
               ffn2_w_down, ffn2_post_g, final_norm_g)
    return (_trunk(x_prompt, weights), _trunk(x_sample, weights))
```

```python
import functools
import math

import jax
import jax.numpy as jnp
from jax import lax
from jax.experimental import pallas as pl
from jax.experimental.pallas import tpu as pltpu

F32 = jnp.float32
BF16 = jnp.bfloat16
ACT = jnp.bfloat16

EPS = 1e-6
MLA_HEADS = 8
MLA_NOPE = 64
MLA_ROPE = 32
MLA_QK = MLA_NOPE + MLA_ROPE
MLA_V = 64
HEAD_PAD = 128
V_ROWS = 80
ROPE_THETA = 10000.0
GDN_HEADS = 4
GDN_D = 128
CONV_K = 5
CHUNK = 64
LANE = 128

ATTN_TQ = 1024
ATTN_TK = 256
ATTN_NB = 32
ATTN_AHEAD = 2
ATTN_SAFE_SHIFT = 40.0
ATTN_BOUND_SLACK = 1.01
GDN_CG = 4

VMEM_LIMIT = 56 * 1024 * 1024


def _dot(a, b):
    return jnp.dot(a, b, preferred_element_type=F32)


def _dot_nt(a, b):
    return lax.dot_general(a, b, (((1,), (1,)), ((), ())), preferred_element_type=F32)


def _rms(x, g):
    ms = jnp.mean(x * x, axis=-1, keepdims=True)
    return x * lax.rsqrt(ms + EPS) * g


def _silu(x):
    return x * jax.nn.sigmoid(x)


def _const_spec(shape):
    nd = len(shape)
    return pl.BlockSpec(shape, lambda *_: (0,) * nd, pipeline_mode=pl.Buffered(1))


def _params(sem):
    return pltpu.CompilerParams(dimension_semantics=sem, vmem_limit_bytes=VMEM_LIMIT)


def _ffn_kernel(x_ref, pre_ref, wg_ref, wu_ref, wd_ref, post_ref, fin_ref, o_ref, *, final):
    x = x_ref[...]
    xn = _rms(x, pre_ref[...]).astype(BF16)
    g = _dot(xn, wg_ref[...])
    u = _dot(xn, wu_ref[...])
    a = (_silu(g) * u).astype(BF16)
    d = _dot(a, wd_ref[...])
    h = x + 0.5 * _rms(d, post_ref[...])
    if final:
        h = _rms(h, fin_ref[...])
    o_ref[...] = h


def _ffn(x2d, pre_g, wg, wu, wd, post_g, fin_g, final, tm):
    T, D = x2d.shape
    F = wg.shape[1]
    row = pl.BlockSpec((tm, D), lambda i: (i, 0))
    return pl.pallas_call(
        functools.partial(_ffn_kernel, final=final),
        grid=(T // tm,),
        in_specs=[row, _const_spec((1, D)), _const_spec((D, F)), _const_spec((D, F)),
                  _const_spec((F, D)), _const_spec((1, D)), _const_spec((1, D))],
        out_specs=row,
        out_shape=jax.ShapeDtypeStruct((T, D), F32),
        compiler_params=_params(("parallel",)),
        name="ffn_final" if final else "ffn",
    )(x2d, pre_g, wg, wu, wd, post_g, fin_g)


def _proj_kernel(h_ref, hp_ref, hn_ref, pre_ref, win_ref, gq_ref, wqa_ref, wqb_ref, gkv_ref, wk_ref, wv_ref,
                 hsel_ref, cw_ref, alog_ref, dtb_ref, cq_ref, sq_ref, ck_ref, sk_ref,
                 qt_ref, k_ref, vt_ref, kn_ref, z_ref, gq_out, gk_out, gv_out, gg_out, *, splits):
    s_cq, s_ckv, s_kr, s_krr, s_qkv, s_z = splits
    u = _rms(h_ref[0], pre_ref[...]).astype(BF16)
    proj = _dot(u, win_ref[...])
    z_ref[0] = proj[:, s_qkv:s_z].astype(z_ref.dtype)

    i = pl.program_id(1)
    halo = jnp.concatenate([hp_ref[0], hn_ref[0]], axis=0)
    qkv_halo = _dot(_rms(halo, pre_ref[...]).astype(BF16), win_ref[:, s_krr:s_qkv])
    prev = jnp.where(i == 0, 0.0, qkv_halo[:8])
    nxt = jnp.where(i == pl.num_programs(1) - 1, 0.0, qkv_halo[8:])
    _gdn_pre(jnp.concatenate([prev, proj[:, s_krr:s_qkv], nxt], axis=0), proj[:, s_z:],
             cw_ref, alog_ref, dtb_ref, gq_out, gk_out, gv_out, gg_out)

    cqn = _rms(proj[:, :s_cq], gq_ref[...]).astype(BF16)
    qa = _dot(cqn, wqa_ref[...])
    qb = _dot(cqn, wqb_ref[...])
    cq, sq = cq_ref[...], sq_ref[...]
    q = jnp.concatenate(
        [qa[:, h * LANE:(h + 1) * LANE] * cq + qb[:, h * LANE:(h + 1) * LANE] * sq
         for h in range(MLA_HEADS)], axis=1)
    qt_ref[0] = q.T.astype(BF16)

    ckvn = _rms(proj[:, s_cq:s_ckv], gkv_ref[...]).astype(BF16)
    kpe = proj[:, s_ckv:s_kr] * ck_ref[...] + proj[:, s_kr:s_krr] * sk_ref[...]
    kb = _dot(ckvn, wk_ref[...])
    kf = jnp.concatenate([kb[:, h * LANE:(h + 1) * LANE] + kpe for h in range(MLA_HEADS)], axis=1)
    k_ref[0] = kf.astype(BF16)
    n2 = _dot((kf * kf).astype(BF16), hsel_ref[...])
    kn_ref[0] = jnp.broadcast_to(jnp.max(n2, axis=0, keepdims=True), (8, LANE))

    vt = _dot(ckvn, wv_ref[...]).T.astype(BF16)
    tm = vt.shape[1]
    ones = jnp.ones((V_ROWS - MLA_V, tm), BF16)
    for h in range(MLA_HEADS):
        vt_ref[0, h * V_ROWS:h * V_ROWS + MLA_V, :] = vt[h * MLA_V:(h + 1) * MLA_V, :]
        vt_ref[0, h * V_ROWS + MLA_V:(h + 1) * V_ROWS, :] = ones


def _proj(h, w, tm):
    B, L, D = h.shape
    ncol = w["w_in"].shape[1]
    HP = MLA_HEADS * HEAD_PAD
    splits = w["splits"]
    n_qkv = splits[4] - splits[3]
    n_z = splits[5] - splits[4]
    nd = GDN_HEADS * GDN_D
    n8 = L // 8
    t8 = tm // 8
    tok = lambda n: pl.BlockSpec((1, tm, n), lambda b, i: (b, i, 0))
    tokT = lambda n: pl.BlockSpec((1, n, tm), lambda b, i: (b, 0, i))
    tab = pl.BlockSpec((tm, LANE), lambda b, i: (i, 0))
    cq_lora = w["wq_a"].shape[0]
    ckv_lora = w["wk"].shape[0]
    return pl.pallas_call(
        functools.partial(_proj_kernel, splits=splits),
        grid=(B, L // tm),
        in_specs=[tok(D),
                  pl.BlockSpec((1, 8, D), lambda b, i: (b, jnp.maximum(i * t8 - 1, 0), 0)),
                  pl.BlockSpec((1, 8, D), lambda b, i: (b, jnp.minimum((i + 1) * t8, n8 - 1), 0)),
                  _const_spec((1, D)), _const_spec((D, ncol)),
                  _const_spec((1, cq_lora)), _const_spec((cq_lora, HP)), _const_spec((cq_lora, HP)),
                  _const_spec((1, ckv_lora)), _const_spec((ckv_lora, HP)),
                  _const_spec((ckv_lora, MLA_HEADS * MLA_V)), _const_spec((HP, LANE)),
                  _const_spec((8, n_qkv)), _const_spec((1, LANE)), _const_spec((1, LANE)),
                  tab, tab, tab, tab],
        out_specs=[tokT(HP), tok(HP), tokT(MLA_HEADS * V_ROWS),
                   pl.BlockSpec((1, 8, LANE), lambda b, i: (b, i, 0)),
                   tok(n_z), tok(nd), tok(nd), tok(nd), tok(LANE)],
        out_shape=[jax.ShapeDtypeStruct((B, HP, L), BF16),
                   jax.ShapeDtypeStruct((B, L, HP), BF16),
                   jax.ShapeDtypeStruct((B, MLA_HEADS * V_ROWS, L), BF16),
                   jax.ShapeDtypeStruct((B, 8 * (L // tm), LANE), F32),
                   jax.ShapeDtypeStruct((B, L, n_z), ACT)]
        + [jax.ShapeDtypeStruct((B, L, nd), F32)] * 3 + [jax.ShapeDtypeStruct((B, L, LANE), F32)],
        compiler_params=_params(("parallel", "parallel")),
        name="proj",
    )(h, h, h, w["mix_pre_g"], w["w_in"], w["q_norm_g"], w["wq_a"], w["wq_b"], w["kv_norm_g"], w["wk"],
      w["wv"], w["head_sel"], w["conv_w"], w["alog"], w["dtb"],
      w["cos_q"], w["sin_q"], w["cos_k"], w["sin_k"])


def _attn_kernel(qt_ref, k_ref, vt_ref, kn_ref, o_ref, *, tk, nb):
    qt = qt_ref[0]
    tq = qt.shape[1]
    step = tk * nb
    nk = k_ref.shape[1] // step
    acc0 = jnp.zeros((V_ROWS, tq), F32)

    def scores(off, i):
        return _dot(k_ref[0, pl.ds(off + i * tk, tk), :], qt)

    qf = qt.astype(F32)
    kn = kn_ref[0]
    head_lane = lax.broadcasted_iota(jnp.int32, kn.shape, 1) == pl.program_id(1)
    k2max = jnp.max(jnp.where(head_lane, kn, 0.0))
    bound = jnp.sqrt(jnp.sum(qf * qf, axis=0, keepdims=True) * k2max) * ATTN_BOUND_SLACK

    def fixed_shift():
        def body(j, acc):
            off = pl.multiple_of(j * step, step)
            ss = [scores(off, i) for i in range(min(ATTN_AHEAD, nb))]
            for i in range(nb):
                if i + ATTN_AHEAD < nb:
                    ss.append(scores(off, i + ATTN_AHEAD))
                p = jnp.exp2(ss[i] - bound).astype(BF16)
                acc = acc + _dot(vt_ref[0, :, pl.ds(off + i * tk, tk)], p)
            return acc
        return lax.fori_loop(0, nk, body, acc0)

    def running_max():
        def body(j, carry):
            m, acc = carry
            off = pl.multiple_of(j * step, step)
            ss = [scores(off, i) for i in range(min(ATTN_AHEAD, nb))]
            for i in range(nb):
                if i + ATTN_AHEAD < nb:
                    ss.append(scores(off, i + ATTN_AHEAD))
                m_new = jnp.maximum(m, jnp.max(ss[i], axis=0, keepdims=True))
                p = jnp.exp2(ss[i] - m_new).astype(BF16)
                acc = acc * jnp.exp2(m - m_new) + _dot(vt_ref[0, :, pl.ds(off + i * tk, tk)], p)
                m = m_new
            return m, acc
        return lax.fori_loop(0, nk, body, (jnp.full((1, tq), -1e30, F32), acc0))[1]

    acc = lax.cond(jnp.max(bound) <= ATTN_SAFE_SHIFT, fixed_shift, running_max)
    o_ref[0] = (acc[:MLA_V, :] / acc[MLA_V:MLA_V + 1, :]).astype(o_ref.dtype)


def _attn(qt, k, vt, kn, tq, tk, nb):
    B, HP, L = qt.shape
    return pl.pallas_call(
        functools.partial(_attn_kernel, tk=tk, nb=nb),
        grid=(B, MLA_HEADS, L // tq),
        in_specs=[pl.BlockSpec((1, HEAD_PAD, tq), lambda b, h, i: (b, h, i)),
                  pl.BlockSpec((1, L, HEAD_PAD), lambda b, h, i: (b, 0, h)),
                  pl.BlockSpec((1, V_ROWS, L), lambda b, h, i: (b, h, 0)),
                  pl.BlockSpec((1, kn.shape[1], LANE), lambda b, h, i: (b, 0, 0))],
        out_specs=pl.BlockSpec((1, MLA_V, tq), lambda b, h, i: (b, h, i)),
        out_shape=jax.ShapeDtypeStruct((B, MLA_HEADS * MLA_V, L), ACT),
        compiler_params=_params(("parallel", "parallel", "parallel")),
        name="attn",
    )(qt, k, vt, kn)


def _gdn_pre(ext, ab, cw_ref, alog_ref, dtb_ref, q_ref, k_ref, v_ref, g_ref):
    tm = ext.shape[0] - 16
    half = CONV_K // 2
    acc = None
    for j in range(CONV_K):
        sh = (half - j) % (tm + 16)
        xs = ext if sh == 0 else pltpu.roll(ext, sh, 0)
        term = xs[8:8 + tm, :] * cw_ref[j:j + 1, :]
        acc = term if acc is None else acc + term
    y = _silu(acc)
    nqk = GDN_HEADS * GDN_D
    for h in range(GDN_HEADS):
        qh = y[:, h * GDN_D:(h + 1) * GDN_D]
        kh = y[:, nqk + h * GDN_D:nqk + (h + 1) * GDN_D]
        qn = qh * lax.rsqrt(jnp.sum(qh * qh, axis=-1, keepdims=True) + EPS) * (GDN_D ** -0.5)
        kn = kh * lax.rsqrt(jnp.sum(kh * kh, axis=-1, keepdims=True) + EPS)
        q_ref[0, :, h * GDN_D:(h + 1) * GDN_D] = qn
        k_ref[0, :, h * GDN_D:(h + 1) * GDN_D] = kn
    v_ref[0] = y[:, 2 * nqk:]
    gate = -jnp.exp(alog_ref[...]) * jax.nn.softplus(ab + dtb_ref[...])
    beta = jax.nn.sigmoid(ab)
    lane = lax.broadcasted_iota(jnp.int32, ab.shape, 1)
    g_ref[0] = jnp.where(lane < 2 * GDN_HEADS, gate, beta)


def _split(x):
    hi = x.astype(BF16)
    lo = (x - hi.astype(F32)).astype(BF16)
    return hi, lo


def _bmm(a, b):
    return lax.dot_general(a, b, (((2,), (1,)), ((0,), (0,))), preferred_element_type=F32)


def _bmm_nt(a, b):
    return lax.dot_general(a, b, (((2,), (2,)), ((0,), (0,))), preferred_element_type=F32)


def _tri_inv(a, eye, level_masks):
    d = eye - jnp.where(level_masks[0], a, 0.0)
    a16 = a.astype(BF16)
    for m in level_masks[1:]:
        db = d.astype(BF16)
        d = d - jnp.where(m, _bmm(db, _bmm(a16, db).astype(BF16)), 0.0)
    mh, ml = _split(eye + a)
    x = d.astype(BF16)
    r = eye - (_bmm(mh, x) + _bmm(ml, x))
    return x.astype(F32) + _bmm(x, r.astype(BF16))


def _gdn_kernel(qf_ref, kf_ref, vf_ref, gf_ref, qb_ref, kb_ref, vb_ref, gb_ref, of_ref, ob_ref, s_ref, *, cg):
    @pl.when(pl.program_id(1) == 0)
    def _():
        s_ref[...] = jnp.zeros_like(s_ref)

    C = CHUNK
    NH = GDN_HEADS
    refs = ((qf_ref, kf_ref, vf_ref, gf_ref, of_ref), (qb_ref, kb_ref, vb_ref, gb_ref, ob_ref))
    row = lax.broadcasted_iota(jnp.int32, (C, C), 0)
    col = lax.broadcasted_iota(jnp.int32, (C, C), 1)
    eye = (row == col).astype(F32)
    level_masks = []
    s = 1
    while s < C:
        level_masks.append(((row // (2 * s)) == (col // (2 * s))) & ((row // s) != (col // s)))
        s *= 2

    def dir_select(n, per_dir, when_bwd, when_fwd):
        idx = lax.broadcasted_iota(jnp.int32, (n, C, C), 0)
        is_bwd = (idx // per_dir) % 2 == 1
        return (is_bwd & when_bwd[None]) | (~is_bwd & when_fwd[None])

    def chunk_rows(j, d):
        c = cg - 1 - j if d else j
        return slice(c * C, (c + 1) * C)

    pairs = [(j, d) for j in range(cg) for d in range(2)]
    units = [(j, d, h) for (j, d) in pairs for h in range(NH)]
    g_all = jnp.stack([refs[d][3][0, chunk_rows(j, d), :] for (j, d) in pairs])
    hi = g_all.astype(BF16)
    r1 = g_all - hi.astype(F32)
    mid = r1.astype(BF16)
    lo = (r1 - mid.astype(F32)).astype(BF16)
    tri = dir_select(len(pairs), 1, row <= col, row >= col).astype(BF16)
    gc_all = _bmm(tri, hi) + (_bmm(tri, mid) + _bmm(tri, lo))
    glast_all = jnp.stack([gc_all[j * 2 + d][(0 if d else C - 1):(1 if d else C), :] for (j, d) in pairs])
    eg_all = jnp.exp(gc_all)
    kdec_all = jnp.exp(glast_all - gc_all)
    dlast_all = jnp.exp(glast_all)
    gc_t = [gc_all[p].T for p in range(len(pairs))]
    g_t = [g_all[p].T for p in range(len(pairs))]

    def per_unit(fn):
        return jnp.stack([fn(j, d, h, j * 2 + d, d * NH + h) for (j, d, h) in units])

    hs = lambda h: slice(h * GDN_D, (h + 1) * GDN_D)
    col1 = lambda arr, c: arr[:, c:c + 1]
    q = per_unit(lambda j, d, h, p, cgate: refs[d][0][0, chunk_rows(j, d), hs(h)])
    k = per_unit(lambda j, d, h, p, cgate: refs[d][1][0, chunk_rows(j, d), hs(h)])
    v = per_unit(lambda j, d, h, p, cgate: refs[d][2][0, chunk_rows(j, d), hs(h)])
    gc = per_unit(lambda j, d, h, p, cgate: col1(gc_all[p], cgate))
    gr = per_unit(lambda j, d, h, p, cgate: gc_t[p][cgate:cgate + 1, :])
    beta = per_unit(lambda j, d, h, p, cgate: col1(g_all[p], 2 * NH + cgate))
    beta_r = per_unit(lambda j, d, h, p, cgate: g_t[p][2 * NH + cgate:2 * NH + cgate + 1, :])
    eg = per_unit(lambda j, d, h, p, cgate: col1(eg_all[p], cgate))
    kdec = per_unit(lambda j, d, h, p, cgate: col1(kdec_all[p], cgate))
    dlast = per_unit(lambda j, d, h, p, cgate: col1(dlast_all[p], cgate))

    U = len(units)
    incl = dir_select(U, NH, row <= col, row >= col)
    strict = dir_select(U, NH, row < col, row > col)
    decay = jnp.where(incl, jnp.exp(jnp.where(incl, gc - gr, 0.0)), 0.0)
    k16 = k.astype(BF16)
    a = jnp.where(strict, _bmm_nt(k16, k16) * decay * beta, 0.0)
    tb = (_tri_inv(a, eye, level_masks) * beta_r).astype(BF16)
    kv = jnp.concatenate([k * eg, v], axis=2).astype(BF16)
    wu = _bmm(tb, kv)
    qk = jnp.where(incl, _bmm_nt(q.astype(BF16), k16) * decay, 0.0).astype(BF16)
    wq = jnp.concatenate([wu[:, :, :GDN_D], q * eg], axis=1).astype(BF16)
    u_part = wu[:, :, GDN_D:]
    kg = k * kdec
    kg_t = jnp.stack([kg[i].T for i in range(U)]).astype(BF16)

    st = s_ref[...]
    n_chain = 2 * NH
    for j in range(cg):
        sl = slice(j * n_chain, (j + 1) * n_chain)
        ws_qs = _bmm(wq[sl], st.astype(BF16))
        v_new = u_part[sl] - ws_qs[:, :C]
        vn16 = v_new.astype(BF16)
        o = ws_qs[:, C:] + _bmm(qk[sl], vn16)
        st = st * dlast[sl] + _bmm(kg_t[sl], vn16)
        for d in range(2):
            for h in range(NH):
                refs[d][4][0, chunk_rows(j, d), hs(h)] = o[d * NH + h].astype(refs[d][4].dtype)
    s_ref[...] = st


def _gdn_scan(q, k, v, g, cg):
    B, L, nd = q.shape
    tm = cg * CHUNK
    n = L // tm
    fwd = lambda w: pl.BlockSpec((1, tm, w), lambda b, i: (b, i, 0))
    bwd = lambda w: pl.BlockSpec((1, tm, w), lambda b, i: (b, n - 1 - i, 0))
    out = jax.ShapeDtypeStruct((B, L, nd), ACT)
    return pl.pallas_call(
        functools.partial(_gdn_kernel, cg=cg),
        grid=(B, n),
        in_specs=[fwd(nd), fwd(nd), fwd(nd), fwd(LANE), bwd(nd), bwd(nd), bwd(nd), bwd(LANE)],
        out_specs=[fwd(nd), bwd(nd)],
        out_shape=[out, out],
        scratch_shapes=[pltpu.VMEM((2 * GDN_HEADS, GDN_D, GDN_D), F32)],
        compiler_params=_params(("parallel", "arbitrary")),
        name="gdn_scan",
    )(q, k, v, g, q, k, v, g)


def _mix_kernel(h_ref, ot_ref, of_ref, ob_ref, z_ref, ga_ref, gg_ref, wa_ref, wb_ref, gp_ref, out_ref):
    ya = _rms(ot_ref[0].astype(F32).T, ga_ref[...]).astype(BF16)
    o = of_ref[0].astype(F32) + ob_ref[0].astype(F32)
    z = z_ref[0].astype(F32)
    gg = gg_ref[...]
    yb = jnp.concatenate(
        [_rms(o[:, h * GDN_D:(h + 1) * GDN_D], gg) * _silu(z[:, h * GDN_D:(h + 1) * GDN_D])
         for h in range(GDN_HEADS)], axis=1).astype(BF16)
    mix = _dot(ya, wa_ref[...]) + _dot(yb, wb_ref[...])
    out_ref[0] = h_ref[0] + _rms(mix, gp_ref[...])


def _mix(h, ot, of, ob, z, w, tm):
    B, L, D = h.shape
    na = ot.shape[1]
    nb = of.shape[2]
    tok = lambda n: pl.BlockSpec((1, tm, n), lambda b, i: (b, i, 0))
    return pl.pallas_call(
        _mix_kernel,
        grid=(B, L // tm),
        in_specs=[tok(D), pl.BlockSpec((1, na, tm), lambda b, i: (b, 0, i)), tok(nb), tok(nb), tok(nb),
                  _const_spec((1, na)), _const_spec((1, GDN_D)), _const_spec((na, D)), _const_spec((nb, D)),
                  _const_spec((1, D))],
        out_specs=tok(D),
        out_shape=jax.ShapeDtypeStruct((B, L, D), F32),
        compiler_params=_params(("parallel", "parallel")),
        name="mix",
    )(h, ot, of, ob, z, w["mla_out_g"], w["gdn_out_g"], w["wo_a"], w["wo_b"], w["mix_post_g"])


def _pad_lanes(a, left, width):
    return jnp.pad(a, ((0, 0), (left, width - left - a.shape[1])))


def _rot_half_cols(wmat):
    half = wmat.shape[-1] // 2
    return jnp.concatenate([-wmat[..., half:], wmat[..., :half]], axis=-1)


def _layer_weights(l, L, ffn1_pre_g, ffn1_w_gate, ffn1_w_up, ffn1_w_down, ffn1_post_g, mix_pre_g, w_in,
                   mla_q_norm_g, mla_w_uq, mla_kv_norm_g, mla_w_ukv, mla_out_norm_g, gdn_conv_w, gdn_a_log,
                   gdn_dt_bias, gdn_out_norm_g, w_out, mix_post_g, ffn2_pre_g, ffn2_w_gate, ffn2_w_up,
                   ffn2_w_down, ffn2_post_g, final_norm_g):
    row = lambda g: g[l][None, :].astype(F32)
    q_lora = mla_w_uq.shape[1]
    kv_lora = mla_w_ukv.shape[1]
    conv_ch = gdn_conv_w.shape[2]
    nz = GDN_HEADS * GDN_D
    c0 = q_lora
    c1 = c0 + kv_lora
    c2 = c1 + MLA_ROPE
    c3 = c2 + conv_ch
    c4 = c3 + nz
    wi = w_in[l]
    w_kr = wi[:, c1:c2]
    w_in_pad = jnp.concatenate([
        wi[:, :c1],
        _pad_lanes(w_kr, MLA_NOPE, LANE),
        _pad_lanes(_rot_half_cols(w_kr), MLA_NOPE, LANE),
        wi[:, c2:c4],
        _pad_lanes(wi[:, c4:], 0, LANE)], axis=1).astype(BF16)
    s_krr = c1 + 2 * LANE
    splits = (c0, c1, c1 + LANE, s_krr, s_krr + conv_ch, s_krr + conv_ch + nz)

    uq = mla_w_uq[l].reshape(q_lora, MLA_HEADS, MLA_QK)
    wq_a = jnp.pad(uq, ((0, 0), (0, 0), (0, HEAD_PAD - MLA_QK))).reshape(q_lora, -1).astype(BF16)
    wq_b = jnp.pad(_rot_half_cols(uq[..., MLA_NOPE:]),
                   ((0, 0), (0, 0), (MLA_NOPE, HEAD_PAD - MLA_QK))).reshape(q_lora, -1).astype(BF16)
    ukv = mla_w_ukv[l].reshape(kv_lora, MLA_HEADS, MLA_NOPE + MLA_V)
    wk = jnp.pad(ukv[..., :MLA_NOPE], ((0, 0), (0, 0), (0, HEAD_PAD - MLA_NOPE))).reshape(kv_lora, -1).astype(BF16)
    wv = ukv[..., MLA_NOPE:].reshape(kv_lora, -1).astype(BF16)

    inv = ROPE_THETA ** (-jnp.arange(0, MLA_ROPE, 2, dtype=F32) / MLA_ROPE)
    ang = jnp.arange(L, dtype=F32)[:, None] * inv[None, :]
    cos, sin = jnp.cos(ang), jnp.sin(ang)
    cos_k = jnp.concatenate([jnp.ones((L, MLA_NOPE), F32), cos, cos,
                             jnp.zeros((L, HEAD_PAD - MLA_QK), F32)], axis=1)
    sin_k = _pad_lanes(jnp.concatenate([sin, sin], axis=1), MLA_NOPE, LANE)
    qscale = (MLA_QK ** -0.5) * math.log2(math.e)

    alog = _pad_lanes(gdn_a_log[l].reshape(1, -1).astype(F32), 0, LANE)
    dtb = _pad_lanes(gdn_dt_bias[l].reshape(1, -1).astype(F32), 0, LANE)
    n_a = mla_out_norm_g.shape[1]
    return dict(
        ffn1=(row(ffn1_pre_g), ffn1_w_gate[l].astype(BF16), ffn1_w_up[l].astype(BF16),
              ffn1_w_down[l].astype(BF16), row(ffn1_post_g)),
        ffn2=(row(ffn2_pre_g), ffn2_w_gate[l].astype(BF16), ffn2_w_up[l].astype(BF16),
              ffn2_w_down[l].astype(BF16), row(ffn2_post_g)),
        final_g=row(final_norm_g),
        mix_pre_g=row(mix_pre_g), w_in=w_in_pad, splits=splits,
        q_norm_g=row(mla_q_norm_g), wq_a=wq_a, wq_b=wq_b, kv_norm_g=row(mla_kv_norm_g), wk=wk, wv=wv,
        cos_q=cos_k * qscale, sin_q=sin_k * qscale, cos_k=cos_k, sin_k=sin_k,
        head_sel=(jnp.arange(MLA_HEADS * HEAD_PAD)[:, None] // HEAD_PAD == jnp.arange(LANE)[None, :]).astype(BF16),
        conv_w=jnp.pad(gdn_conv_w[l].astype(F32), ((0, 8 - CONV_K), (0, 0))), alog=alog, dtb=dtb,
        mla_out_g=row(mla_out_norm_g), gdn_out_g=row(gdn_out_norm_g),
        wo_a=w_out[l][:n_a].astype(BF16), wo_b=w_out[l][n_a:].astype(BF16), mix_post_g=row(mix_post_g),
    )


def _tile(L, pref):
    return pref if L % pref == 0 else L


def _layer(h, w, last):
    B, L, D = h.shape
    tm = _tile(L, 512)
    h = _ffn(h.reshape(B * L, D), *w["ffn1"], w["final_g"], False, tm).reshape(B, L, D)
    qt, k, vt, kn, z, gq, gk, gv, gg = _proj(h, w, tm)
    tk = _tile(L, ATTN_TK)
    nb = ATTN_NB
    while (L // tk) % nb:
        nb //= 2
    ot = _attn(qt, k, vt, kn, _tile(L, ATTN_TQ), tk, nb)
    of, ob = _gdn_scan(gq, gk, gv, gg, GDN_CG)
    h = _mix(h, ot, of, ob, z, w, tm)
    h = _ffn(h.reshape(B * L, D), *w["ffn2"], w["final_g"], last, tm).reshape(B, L, D)
    return h


def _trunk(x, weights):
    depth = weights[0].shape[0]
    h = x
    for l in range(depth):
        h = _layer(h, _layer_weights(l, x.shape[1], *weights), l == depth - 1)
    return h


def kernel(x_prompt, x_sample, ffn1_pre_g, ffn1_w_gate, ffn1_w_up, ffn1_w_down, ffn1_post_g, mix_pre_g, w_in, mla_q_norm_g, mla_w_uq, mla_kv_norm_g, mla_w_ukv, mla_out_norm_g, gdn_conv_w, gdn_a_log, gdn_dt_bias, gdn_out_norm_g, w_out, mix_post_g, ffn2_pre_g, ffn2_w_gate, ffn2_w_up, ffn2_w_down, ffn2_post_g, final_norm_g):
    weights = (ffn1_pre_g, ffn1_w_gate, ffn1_w_up, ffn1_w_down, ffn1_post_g, mix_pre_g, w_in,
               mla_q_norm_g, mla_w_uq, mla_kv_norm_g, mla_w_ukv, mla_out_norm_g, gdn_conv_w, gdn_a_log,
               gdn_dt_bias, gdn_out_norm_g, w_out, mix_post_g, ffn2_pre_g, ffn2_w_gate, ffn2_w_up,
               ffn2_w_down, ffn2_post_g, final_norm_g)
    return (_trunk(x_prompt, weights), _trunk(x_sample, weights))
```

```python
import functools
import math

import jax
import jax.numpy as jnp
from jax import lax
from jax.experimental import pallas as pl
from jax.experimental.pallas import tpu as pltpu

F32 = jnp.float32
BF16 = jnp.bfloat16
ACT = jnp.bfloat16

EPS = 1e-6
MLA_HEADS = 8
MLA_NOPE = 64
MLA_ROPE = 32
MLA_QK = MLA_NOPE + MLA_ROPE
MLA_V = 64
HEAD_PAD = 128
V_ROWS = 80
ROPE_THETA = 10000.0
GDN_HEADS = 4
GDN_D = 128
CONV_K = 5
CHUNK = 64
LANE = 128

ATTN_TQ = 1024
ATTN_TQ_FULL_L = 8192
ATTN_TK = 256
ATTN_NB = 32
ATTN_AHEAD = 2
ATTN_SAFE_SHIFT = 40.0
ATTN_BOUND_SLACK = 1.01
GDN_CG = 4
MIX_TM = 1024

VMEM_LIMIT = 56 * 1024 * 1024


def _dot(a, b):
    return jnp.dot(a, b, preferred_element_type=F32)


def _dot_nt(a, b):
    return lax.dot_general(a, b, (((1,), (1,)), ((), ())), preferred_element_type=F32)


def _rms(x, g):
    ms = jnp.mean(x * x, axis=-1, keepdims=True)
    return x * lax.rsqrt(ms + EPS) * g


def _silu(x):
    return x * jax.nn.sigmoid(x)


def _const_spec(shape):
    nd = len(shape)
    return pl.BlockSpec(shape, lambda *_: (0,) * nd, pipeline_mode=pl.Buffered(1))


def _params(sem):
    return pltpu.CompilerParams(dimension_semantics=sem, vmem_limit_bytes=VMEM_LIMIT)


def _ffn_kernel(x_ref, pre_ref, wg_ref, wu_ref, wd_ref, post_ref, fin_ref, o_ref, *, final):
    x = x_ref[...]
    xn = _rms(x, pre_ref[...]).astype(BF16)
    g = _dot(xn, wg_ref[...])
    u = _dot(xn, wu_ref[...])
    a = (_silu(g) * u).astype(BF16)
    d = _dot(a, wd_ref[...])
    h = x + 0.5 * _rms(d, post_ref[...])
    if final:
        h = _rms(h, fin_ref[...])
    o_ref[...] = h


def _ffn(x2d, pre_g, wg, wu, wd, post_g, fin_g, final, tm):
    T, D = x2d.shape
    F = wg.shape[1]
    row = pl.BlockSpec((tm, D), lambda i: (i, 0))
    return pl.pallas_call(
        functools.partial(_ffn_kernel, final=final),
        grid=(T // tm,),
        in_specs=[row, _const_spec((1, D)), _const_spec((D, F)), _const_spec((D, F)),
                  _const_spec((F, D)), _const_spec((1, D)), _const_spec((1, D))],
        out_specs=row,
        out_shape=jax.ShapeDtypeStruct((T, D), F32),
        compiler_params=_params(("parallel",)),
        name="ffn_final" if final else "ffn",
    )(x2d, pre_g, wg, wu, wd, post_g, fin_g)


def _proj_kernel(h_ref, hp_ref, hn_ref, pre_ref, win_ref, gq_ref, wqa_ref, wqb_ref, gkv_ref, wk_ref, wv_ref,
                 hsel_ref, cw_ref, alog_ref, dtb_ref, cq_ref, sq_ref, ck_ref, sk_ref,
                 qt_ref, k_ref, vt_ref, kn_ref, z_ref, gq_out, gk_out, gv_out, gg_out, *, splits):
    s_cq, s_ckv, s_kr, s_krr, s_qkv, s_z = splits
    u = _rms(h_ref[0], pre_ref[...]).astype(BF16)
    proj = _dot(u, win_ref[...])
    z_ref[0] = proj[:, s_qkv:s_z].astype(z_ref.dtype)

    i = pl.program_id(1)
    halo = jnp.concatenate([hp_ref[0], hn_ref[0]], axis=0)
    qkv_halo = _dot(_rms(halo, pre_ref[...]).astype(BF16), win_ref[:, s_krr:s_qkv])
    prev = jnp.where(i == 0, 0.0, qkv_halo[:8])
    nxt = jnp.where(i == pl.num_programs(1) - 1, 0.0, qkv_halo[8:])
    _gdn_pre(jnp.concatenate([prev, proj[:, s_krr:s_qkv], nxt], axis=0), proj[:, s_z:],
             cw_ref, alog_ref, dtb_ref, gq_out, gk_out, gv_out, gg_out)

    cqn = _rms(proj[:, :s_cq], gq_ref[...]).astype(BF16)
    qa = _dot(cqn, wqa_ref[...])
    qb = _dot(cqn, wqb_ref[...])
    cq, sq = cq_ref[...], sq_ref[...]
    q = jnp.concatenate(
        [qa[:, h * LANE:(h + 1) * LANE] * cq + qb[:, h * LANE:(h + 1) * LANE] * sq
         for h in range(MLA_HEADS)], axis=1)
    qt_ref[0] = q.T.astype(BF16)

    ckvn = _rms(proj[:, s_cq:s_ckv], gkv_ref[...]).astype(BF16)
    kpe = proj[:, s_ckv:s_kr] * ck_ref[...] + proj[:, s_kr:s_krr] * sk_ref[...]
    kb = _dot(ckvn, wk_ref[...])
    kf = jnp.concatenate([kb[:, h * LANE:(h + 1) * LANE] + kpe for h in range(MLA_HEADS)], axis=1)
    k_ref[0] = kf.astype(BF16)
    n2 = _dot((kf * kf).astype(BF16), hsel_ref[...])
    kn_ref[0] = jnp.broadcast_to(jnp.max(n2, axis=0, keepdims=True), (8, LANE))

    vt = _dot(ckvn, wv_ref[...]).T.astype(BF16)
    tm = vt.shape[1]
    ones = jnp.ones((V_ROWS - MLA_V, tm), BF16)
    for h in range(MLA_HEADS):
        vt_ref[0, h * V_ROWS:h * V_ROWS + MLA_V, :] = vt[h * MLA_V:(h + 1) * MLA_V, :]
        vt_ref[0, h * V_ROWS + MLA_V:(h + 1) * V_ROWS, :] = ones


def _proj(h, w, tm):
    B, L, D = h.shape
    ncol = w["w_in"].shape[1]
    HP = MLA_HEADS * HEAD_PAD
    splits = w["splits"]
    n_qkv = splits[4] - splits[3]
    n_z = splits[5] - splits[4]
    nd = GDN_HEADS * GDN_D
    n8 = L // 8
    t8 = tm // 8
    tok = lambda n: pl.BlockSpec((1, tm, n), lambda b, i: (b, i, 0))
    tokT = lambda n: pl.BlockSpec((1, n, tm), lambda b, i: (b, 0, i))
    tab = pl.BlockSpec((tm, LANE), lambda b, i: (i, 0))
    cq_lora = w["wq_a"].shape[0]
    ckv_lora = w["wk"].shape[0]
    return pl.pallas_call(
        functools.partial(_proj_kernel, splits=splits),
        grid=(B, L // tm),
        in_specs=[tok(D),
                  pl.BlockSpec((1, 8, D), lambda b, i: (b, jnp.maximum(i * t8 - 1, 0), 0)),
                  pl.BlockSpec((1, 8, D), lambda b, i: (b, jnp.minimum((i + 1) * t8, n8 - 1), 0)),
                  _const_spec((1, D)), _const_spec((D, ncol)),
                  _const_spec((1, cq_lora)), _const_spec((cq_lora, HP)), _const_spec((cq_lora, HP)),
                  _const_spec((1, ckv_lora)), _const_spec((ckv_lora, HP)),
                  _const_spec((ckv_lora, MLA_HEADS * MLA_V)), _const_spec((HP, LANE)),
                  _const_spec((8, n_qkv)), _const_spec((1, LANE)), _const_spec((1, LANE)),
                  tab, tab, tab, tab],
        out_specs=[tokT(HP), tok(HP), tokT(MLA_HEADS * V_ROWS),
                   pl.BlockSpec((1, 8, LANE), lambda b, i: (b, i, 0)),
                   tok(n_z), tok(nd), tok(nd), tok(nd), tok(LANE)],
        out_shape=[jax.ShapeDtypeStruct((B, HP, L), BF16),
                   jax.ShapeDtypeStruct((B, L, HP), BF16),
                   jax.ShapeDtypeStruct((B, MLA_HEADS * V_ROWS, L), BF16),
                   jax.ShapeDtypeStruct((B, 8 * (L // tm), LANE), F32),
                   jax.ShapeDtypeStruct((B, L, n_z), ACT)]
        + [jax.ShapeDtypeStruct((B, L, nd), F32)] * 3 + [jax.ShapeDtypeStruct((B, L, LANE), F32)],
        compiler_params=_params(("parallel", "parallel")),
        name="proj",
    )(h, h, h, w["mix_pre_g"], w["w_in"], w["q_norm_g"], w["wq_a"], w["wq_b"], w["kv_norm_g"], w["wk"],
      w["wv"], w["head_sel"], w["conv_w"], w["alog"], w["dtb"],
      w["cos_q"], w["sin_q"], w["cos_k"], w["sin_k"])


def _attn_kernel(qt_ref, k_ref, vt_ref, kn_ref, o_ref, *, tk, nb):
    qt = qt_ref[0]
    tq = qt.shape[1]
    step = tk * nb
    nk = k_ref.shape[1] // step
    acc0 = jnp.zeros((V_ROWS, tq), F32)

    def scores(off, i):
        return _dot(k_ref[0, pl.ds(off + i * tk, tk), :], qt)

    qf = qt.astype(F32)
    kn = kn_ref[0]
    head_lane = lax.broadcasted_iota(jnp.int32, kn.shape, 1) == pl.program_id(1)
    k2max = jnp.max(jnp.where(head_lane, kn, 0.0))
    bound = jnp.sqrt(jnp.sum(qf * qf, axis=0, keepdims=True) * k2max) * ATTN_BOUND_SLACK

    def fixed_shift():
        def body(j, acc):
            off = pl.multiple_of(j * step, step)
            ss = [scores(off, i) for i in range(min(ATTN_AHEAD, nb))]
            for i in range(nb):
                if i + ATTN_AHEAD < nb:
                    ss.append(scores(off, i + ATTN_AHEAD))
                p = jnp.exp2(ss[i] - bound).astype(BF16)
                acc = acc + _dot(vt_ref[0, :, pl.ds(off + i * tk, tk)], p)
            return acc
        return lax.fori_loop(0, nk, body, acc0)

    def running_max():
        def body(j, carry):
            m, acc = carry
            off = pl.multiple_of(j * step, step)
            ss = [scores(off, i) for i in range(min(ATTN_AHEAD, nb))]
            for i in range(nb):
                if i + ATTN_AHEAD < nb:
                    ss.append(scores(off, i + ATTN_AHEAD))
                m_new = jnp.maximum(m, jnp.max(ss[i], axis=0, keepdims=True))
                p = jnp.exp2(ss[i] - m_new).astype(BF16)
                acc = acc * jnp.exp2(m - m_new) + _dot(vt_ref[0, :, pl.ds(off + i * tk, tk)], p)
                m = m_new
            return m, acc
        return lax.fori_loop(0, nk, body, (jnp.full((1, tq), -1e30, F32), acc0))[1]

    acc = lax.cond(jnp.max(bound) <= ATTN_SAFE_SHIFT, fixed_shift, running_max)
    o_ref[0] = (acc[:MLA_V, :] / acc[MLA_V:MLA_V + 1, :]).astype(o_ref.dtype)


def _attn(qt, k, vt, kn, tq, tk, nb):
    B, HP, L = qt.shape
    return pl.pallas_call(
        functools.partial(_attn_kernel, tk=tk, nb=nb),
        grid=(B, MLA_HEADS, L // tq),
        in_specs=[pl.BlockSpec((1, HEAD_PAD, tq), lambda b, h, i: (b, h, i)),
                  pl.BlockSpec((1, L, HEAD_PAD), lambda b, h, i: (b, 0, h)),
                  pl.BlockSpec((1, V_ROWS, L), lambda b, h, i: (b, h, 0)),
                  pl.BlockSpec((1, kn.shape[1], LANE), lambda b, h, i: (b, 0, 0))],
        out_specs=pl.BlockSpec((1, MLA_V, tq), lambda b, h, i: (b, h, i)),
        out_shape=jax.ShapeDtypeStruct((B, MLA_HEADS * MLA_V, L), ACT),
        compiler_params=_params(("parallel", "parallel", "parallel")),
        name="attn",
    )(qt, k, vt, kn)


def _gdn_pre(ext, ab, cw_ref, alog_ref, dtb_ref, q_ref, k_ref, v_ref, g_ref):
    tm = ext.shape[0] - 16
    half = CONV_K // 2
    acc = None
    for j in range(CONV_K):
        sh = (half - j) % (tm + 16)
        xs = ext if sh == 0 else pltpu.roll(ext, sh, 0)
        term = xs[8:8 + tm, :] * cw_ref[j:j + 1, :]
        acc = term if acc is None else acc + term
    y = _silu(acc)
    nqk = GDN_HEADS * GDN_D
    for h in range(GDN_HEADS):
        qh = y[:, h * GDN_D:(h + 1) * GDN_D]
        kh = y[:, nqk + h * GDN_D:nqk + (h + 1) * GDN_D]
        qn = qh * lax.rsqrt(jnp.sum(qh * qh, axis=-1, keepdims=True) + EPS) * (GDN_D ** -0.5)
        kn = kh * lax.rsqrt(jnp.sum(kh * kh, axis=-1, keepdims=True) + EPS)
        q_ref[0, :, h * GDN_D:(h + 1) * GDN_D] = qn
        k_ref[0, :, h * GDN_D:(h + 1) * GDN_D] = kn
    v_ref[0] = y[:, 2 * nqk:]
    gate = -jnp.exp(alog_ref[...]) * jax.nn.softplus(ab + dtb_ref[...])
    beta = jax.nn.sigmoid(ab)
    lane = lax.broadcasted_iota(jnp.int32, ab.shape, 1)
    g_ref[0] = jnp.where(lane < 2 * GDN_HEADS, gate, beta)


def _split(x):
    hi = x.astype(BF16)
    lo = (x - hi.astype(F32)).astype(BF16)
    return hi, lo


def _bmm(a, b):
    return lax.dot_general(a, b, (((2,), (1,)), ((0,), (0,))), preferred_element_type=F32)


def _bmm_nt(a, b):
    return lax.dot_general(a, b, (((2,), (2,)), ((0,), (0,))), preferred_element_type=F32)


def _tri_inv(a, eye, level_masks):
    d = eye - jnp.where(level_masks[0], a, 0.0)
    a16 = a.astype(BF16)
    for m in level_masks[1:]:
        db = d.astype(BF16)
        d = d - jnp.where(m, _bmm(db, _bmm(a16, db).astype(BF16)), 0.0)
    mh, ml = _split(eye + a)
    x = d.astype(BF16)
    r = eye - (_bmm(mh, x) + _bmm(ml, x))
    return x.astype(F32) + _bmm(x, r.astype(BF16))


def _gdn_kernel(qf_ref, kf_ref, vf_ref, gf_ref, qb_ref, kb_ref, vb_ref, gb_ref, of_ref, ob_ref, s_ref, *, cg):
    @pl.when(pl.program_id(1) == 0)
    def _():
        s_ref[...] = jnp.zeros_like(s_ref)

    C = CHUNK
    NH = GDN_HEADS
    refs = ((qf_ref, kf_ref, vf_ref, gf_ref, of_ref), (qb_ref, kb_ref, vb_ref, gb_ref, ob_ref))
    row = lax.broadcasted_iota(jnp.int32, (C, C), 0)
    col = lax.broadcasted_iota(jnp.int32, (C, C), 1)
    eye = (row == col).astype(F32)
    level_masks = []
    s = 1
    while s < C:
        level_masks.append(((row // (2 * s)) == (col // (2 * s))) & ((row // s) != (col // s)))
        s *= 2

    def dir_select(n, per_dir, when_bwd, when_fwd):
        idx = lax.broadcasted_iota(jnp.int32, (n, C, C), 0)
        is_bwd = (idx // per_dir) % 2 == 1
        return (is_bwd & when_bwd[None]) | (~is_bwd & when_fwd[None])

    def chunk_rows(j, d):
        c = cg - 1 - j if d else j
        return slice(c * C, (c + 1) * C)

    pairs = [(j, d) for j in range(cg) for d in range(2)]
    units = [(j, d, h) for (j, d) in pairs for h in range(NH)]
    g_all = jnp.stack([refs[d][3][0, chunk_rows(j, d), :] for (j, d) in pairs])
    hi = g_all.astype(BF16)
    r1 = g_all - hi.astype(F32)
    mid = r1.astype(BF16)
    lo = (r1 - mid.astype(F32)).astype(BF16)
    tri = dir_select(len(pairs), 1, row <= col, row >= col).astype(BF16)
    gc_all = _bmm(tri, hi) + (_bmm(tri, mid) + _bmm(tri, lo))
    glast_all = jnp.stack([gc_all[j * 2 + d][(0 if d else C - 1):(1 if d else C), :] for (j, d) in pairs])
    eg_all = jnp.exp(gc_all)
    kdec_all = jnp.exp(glast_all - gc_all)
    dlast_all = jnp.exp(glast_all)
    gc_t = [gc_all[p].T for p in range(len(pairs))]
    g_t = [g_all[p].T for p in range(len(pairs))]

    def per_unit(fn):
        return jnp.stack([fn(j, d, h, j * 2 + d, d * NH + h) for (j, d, h) in units])

    hs = lambda h: slice(h * GDN_D, (h + 1) * GDN_D)
    col1 = lambda arr, c: arr[:, c:c + 1]
    q = per_unit(lambda j, d, h, p, cgate: refs[d][0][0, chunk_rows(j, d), hs(h)])
    k = per_unit(lambda j, d, h, p, cgate: refs[d][1][0, chunk_rows(j, d), hs(h)])
    v = per_unit(lambda j, d, h, p, cgate: refs[d][2][0, chunk_rows(j, d), hs(h)])
    gc = per_unit(lambda j, d, h, p, cgate: col1(gc_all[p], cgate))
    gr = per_unit(lambda j, d, h, p, cgate: gc_t[p][cgate:cgate + 1, :])
    beta = per_unit(lambda j, d, h, p, cgate: col1(g_all[p], 2 * NH + cgate))
    beta_r = per_unit(lambda j, d, h, p, cgate: g_t[p][2 * NH + cgate:2 * NH + cgate + 1, :])
    eg = per_unit(lambda j, d, h, p, cgate: col1(eg_all[p], cgate))
    kdec = per_unit(lambda j, d, h, p, cgate: col1(kdec_all[p], cgate))
    dlast = per_unit(lambda j, d, h, p, cgate: col1(dlast_all[p], cgate))

    U = len(units)
    incl = dir_select(U, NH, row <= col, row >= col)
    strict = dir_select(U, NH, row < col, row > col)
    decay = jnp.where(incl, jnp.exp(jnp.where(incl, gc - gr, 0.0)), 0.0)
    k16 = k.astype(BF16)
    a = jnp.where(strict, _bmm_nt(k16, k16) * decay * beta, 0.0)
    tb = (_tri_inv(a, eye, level_masks) * beta_r).astype(BF16)
    kv = jnp.concatenate([k * eg, v], axis=2).astype(BF16)
    wu = _bmm(tb, kv)
    qk = jnp.where(incl, _bmm_nt(q.astype(BF16), k16) * decay, 0.0).astype(BF16)
    wq = jnp.concatenate([wu[:, :, :GDN_D], q * eg], axis=1).astype(BF16)
    u_part = wu[:, :, GDN_D:]
    kg = k * kdec
    kg_t = jnp.stack([kg[i].T for i in range(U)]).astype(BF16)

    st = s_ref[...]
    n_chain = 2 * NH
    for j in range(cg):
        sl = slice(j * n_chain, (j + 1) * n_chain)
        ws_qs = _bmm(wq[sl], st.astype(BF16))
        v_new = u_part[sl] - ws_qs[:, :C]
        vn16 = v_new.astype(BF16)
        o = ws_qs[:, C:] + _bmm(qk[sl], vn16)
        st = st * dlast[sl] + _bmm(kg_t[sl], vn16)
        for d in range(2):
            for h in range(NH):
                refs[d][4][0, chunk_rows(j, d), hs(h)] = o[d * NH + h].astype(refs[d][4].dtype)
    s_ref[...] = st


def _gdn_scan(q, k, v, g, cg):
    B, L, nd = q.shape
    tm = cg * CHUNK
    n = L // tm
    fwd = lambda w: pl.BlockSpec((1, tm, w), lambda b, i: (b, i, 0))
    bwd = lambda w: pl.BlockSpec((1, tm, w), lambda b, i: (b, n - 1 - i, 0))
    out = jax.ShapeDtypeStruct((B, L, nd), ACT)
    return pl.pallas_call(
        functools.partial(_gdn_kernel, cg=cg),
        grid=(B, n),
        in_specs=[fwd(nd), fwd(nd), fwd(nd), fwd(LANE), bwd(nd), bwd(nd), bwd(nd), bwd(LANE)],
        out_specs=[fwd(nd), bwd(nd)],
        out_shape=[out, out],
        scratch_shapes=[pltpu.VMEM((2 * GDN_HEADS, GDN_D, GDN_D), F32)],
        compiler_params=_params(("parallel", "arbitrary")),
        name="gdn_scan",
    )(q, k, v, g, q, k, v, g)


def _mix_kernel(h_ref, ot_ref, of_ref, ob_ref, z_ref, ga_ref, gg_ref, wa_ref, wb_ref, gp_ref, out_ref):
    ya = _rms(ot_ref[0].astype(F32).T, ga_ref[...]).astype(BF16)
    o = of_ref[0].astype(F32) + ob_ref[0].astype(F32)
    z = z_ref[0].astype(F32)
    gg = gg_ref[...]
    yb = jnp.concatenate(
        [_rms(o[:, h * GDN_D:(h + 1) * GDN_D], gg) * _silu(z[:, h * GDN_D:(h + 1) * GDN_D])
         for h in range(GDN_HEADS)], axis=1).astype(BF16)
    mix = _dot(ya, wa_ref[...]) + _dot(yb, wb_ref[...])
    out_ref[0] = h_ref[0] + _rms(mix, gp_ref[...])


def _mix(h, ot, of, ob, z, w, tm):
    B, L, D = h.shape
    na = ot.shape[1]
    nb = of.shape[2]
    tok = lambda n: pl.BlockSpec((1, tm, n), lambda b, i: (b, i, 0))
    return pl.pallas_call(
        _mix_kernel,
        grid=(B, L // tm),
        in_specs=[tok(D), pl.BlockSpec((1, na, tm), lambda b, i: (b, 0, i)), tok(nb), tok(nb), tok(nb),
                  _const_spec((1, na)), _const_spec((1, GDN_D)), _const_spec((na, D)), _const_spec((nb, D)),
                  _const_spec((1, D))],
        out_specs=tok(D),
        out_shape=jax.ShapeDtypeStruct((B, L, D), F32),
        compiler_params=_params(("parallel", "parallel")),
        name="mix",
    )(h, ot, of, ob, z, w["mla_out_g"], w["gdn_out_g"], w["wo_a"], w["wo_b"], w["mix_post_g"])


def _pad_lanes(a, left, width):
    return jnp.pad(a, ((0, 0), (left, width - left - a.shape[1])))


def _rot_half_cols(wmat):
    half = wmat.shape[-1] // 2
    return jnp.concatenate([-wmat[..., half:], wmat[..., :half]], axis=-1)


def _layer_weights(l, L, ffn1_pre_g, ffn1_w_gate, ffn1_w_up, ffn1_w_down, ffn1_post_g, mix_pre_g, w_in,
                   mla_q_norm_g, mla_w_uq, mla_kv_norm_g, mla_w_ukv, mla_out_norm_g, gdn_conv_w, gdn_a_log,
                   gdn_dt_bias, gdn_out_norm_g, w_out, mix_post_g, ffn2_pre_g, ffn2_w_gate, ffn2_w_up,
                   ffn2_w_down, ffn2_post_g, final_norm_g):
    row = lambda g: g[l][None, :].astype(F32)
    q_lora = mla_w_uq.shape[1]
    kv_lora = mla_w_ukv.shape[1]
    conv_ch = gdn_conv_w.shape[2]
    nz = GDN_HEADS * GDN_D
    c0 = q_lora
    c1 = c0 + kv_lora
    c2 = c1 + MLA_ROPE
    c3 = c2 + conv_ch
    c4 = c3 + nz
    wi = w_in[l]
    w_kr = wi[:, c1:c2]
    w_in_pad = jnp.concatenate([
        wi[:, :c1],
        _pad_lanes(w_kr, MLA_NOPE, LANE),
        _pad_lanes(_rot_half_cols(w_kr), MLA_NOPE, LANE),
        wi[:, c2:c4],
        _pad_lanes(wi[:, c4:], 0, LANE)], axis=1).astype(BF16)
    s_krr = c1 + 2 * LANE
    splits = (c0, c1, c1 + LANE, s_krr, s_krr + conv_ch, s_krr + conv_ch + nz)

    uq = mla_w_uq[l].reshape(q_lora, MLA_HEADS, MLA_QK)
    wq_a = jnp.pad(uq, ((0, 0), (0, 0), (0, HEAD_PAD - MLA_QK))).reshape(q_lora, -1).astype(BF16)
    wq_b = jnp.pad(_rot_half_cols(uq[..., MLA_NOPE:]),
                   ((0, 0), (0, 0), (MLA_NOPE, HEAD_PAD - MLA_QK))).reshape(q_lora, -1).astype(BF16)
    ukv = mla_w_ukv[l].reshape(kv_lora, MLA_HEADS, MLA_NOPE + MLA_V)
    wk = jnp.pad(ukv[..., :MLA_NOPE], ((0, 0), (0, 0), (0, HEAD_PAD - MLA_NOPE))).reshape(kv_lora, -1).astype(BF16)
    wv = ukv[..., MLA_NOPE:].reshape(kv_lora, -1).astype(BF16)

    inv = ROPE_THETA ** (-jnp.arange(0, MLA_ROPE, 2, dtype=F32) / MLA_ROPE)
    ang = jnp.arange(L, dtype=F32)[:, None] * inv[None, :]
    cos, sin = jnp.cos(ang), jnp.sin(ang)
    cos_k = jnp.concatenate([jnp.ones((L, MLA_NOPE), F32), cos, cos,
                             jnp.zeros((L, HEAD_PAD - MLA_QK), F32)], axis=1)
    sin_k = _pad_lanes(jnp.concatenate([sin, sin], axis=1), MLA_NOPE, LANE)
    qscale = (MLA_QK ** -0.5) * math.log2(math.e)

    alog = _pad_lanes(gdn_a_log[l].reshape(1, -1).astype(F32), 0, LANE)
    dtb = _pad_lanes(gdn_dt_bias[l].reshape(1, -1).astype(F32), 0, LANE)
    n_a = mla_out_norm_g.shape[1]
    return dict(
        ffn1=(row(ffn1_pre_g), ffn1_w_gate[l].astype(BF16), ffn1_w_up[l].astype(BF16),
              ffn1_w_down[l].astype(BF16), row(ffn1_post_g)),
        ffn2=(row(ffn2_pre_g), ffn2_w_gate[l].astype(BF16), ffn2_w_up[l].astype(BF16),
              ffn2_w_down[l].astype(BF16), row(ffn2_post_g)),
        final_g=row(final_norm_g),
        mix_pre_g=row(mix_pre_g), w_in=w_in_pad, splits=splits,
        q_norm_g=row(mla_q_norm_g), wq_a=wq_a, wq_b=wq_b, kv_norm_g=row(mla_kv_norm_g), wk=wk, wv=wv,
        cos_q=cos_k * qscale, sin_q=sin_k * qscale, cos_k=cos_k, sin_k=sin_k,
        head_sel=(jnp.arange(MLA_HEADS * HEAD_PAD)[:, None] // HEAD_PAD == jnp.arange(LANE)[None, :]).astype(BF16),
        conv_w=jnp.pad(gdn_conv_w[l].astype(F32), ((0, 8 - CONV_K), (0, 0))), alog=alog, dtb=dtb,
        mla_out_g=row(mla_out_norm_g), gdn_out_g=row(gdn_out_norm_g),
        wo_a=w_out[l][:n_a].astype(BF16), wo_b=w_out[l][n_a:].astype(BF16), mix_post_g=row(mix_post_g),
    )


def _tile(L, pref):
    return pref if L % pref == 0 else L


def _layer(h, w, last):
    B, L, D = h.shape
    tm = _tile(L, 512)
    h = _ffn(h.reshape(B * L, D), *w["ffn1"], w["final_g"], False, tm).reshape(B, L, D)
    qt, k, vt, kn, z, gq, gk, gv, gg = _proj(h, w, tm)
    tk = _tile(L, ATTN_TK)
    nb = ATTN_NB
    while (L // tk) % nb:
        nb //= 2
    tq = _tile(L, min(L, ATTN_TQ * max(1, ATTN_TQ_FULL_L // L)))
    ot = _attn(qt, k, vt, kn, tq, tk, nb)
    of, ob = _gdn_scan(gq, gk, gv, gg, GDN_CG)
    h = _mix(h, ot, of, ob, z, w, _tile(L, MIX_TM))
    h = _ffn(h.reshape(B * L, D), *w["ffn2"], w["final_g"], last, tm).reshape(B, L, D)
    return h


def _trunk(x, weights):
    depth = weights[0].shape[0]
    h = x
    for l in range(depth):
        h = _layer(h, _layer_weights(l, x.shape[1], *weights), l == depth - 1)
    return h


def kernel(x_prompt, x_sample, ffn1_pre_g, ffn1_w_gate, ffn1_w_up, ffn1_w_down, ffn1_post_g, mix_pre_g, w_in, mla_q_norm_g, mla_w_uq, mla_kv_norm_g, mla_w_ukv, mla_out_norm_g, gdn_conv_w, gdn_a_log, gdn_dt_bias, gdn_out_norm_g, w_out, mix_post_g, ffn2_pre_g, ffn2_w_gate, ffn2_w_up, ffn2_w_down, ffn2_post_g, final_norm_g):
    weights = (ffn1_pre_g, ffn1_w_gate, ffn1_w_up, ffn1_w_down, ffn1_post_g, mix_pre_g, w_in,
               mla_q_norm_g, mla_w_uq, mla_kv_norm_g, mla_w_ukv, mla_out_norm_g, gdn_conv_w, gdn_a_log,
               gdn_dt_bias, gdn_out_norm_g, w_out, mix_post_g, ffn2_pre_g, ffn2_w_gate, ffn2_w_up,
               ffn2_w_down, ffn2_post_g, final_norm_g)
    return (_trunk(x_prompt, weights), _trunk(x_sample, weights))
```
